```python
import math
import jax, jax.numpy as jnp
from jax import lax
import numpy as np

D_MODEL = 1024
BATCH = 4
SEQ = 4096
DEPTH = 1
DEC_BATCH = 8
DEC_SEQ = 8192
PAST_LEN = 128

EPS = 1e-6
N_MEM = 256
GLA_HEADS = 4
GLA_DV = D_MODEL // GLA_HEADS
GLA_DK = GLA_DV // 2
GLA_QK = GLA_HEADS * GLA_DK
GLA_V = GLA_HEADS * GLA_DV
GLA_RANK = 16
GLA_NORMALIZER = 16.0
GLA_CHUNK = 64
GDN_HEADS = 4
GDN_DV = D_MODEL // GDN_HEADS
GDN_DK = GDN_DV // 2
GDN_QK = GDN_HEADS * GDN_DK
GDN_V = GDN_HEADS * GDN_DV
GDN_CONV = 3
GDN_CHUNK = 64
IN_SPLITS = (GLA_QK, GLA_QK, GLA_V, GLA_V, GLA_RANK, GLA_RANK,
             GDN_QK, GDN_QK, GDN_V, GDN_V, GDN_HEADS, GDN_HEADS, GDN_HEADS, GDN_HEADS,
             D_MODEL, D_MODEL)
IN_WIDTH = 2 * GLA_QK + 2 * GLA_V + 2 * GLA_RANK + 2 * GDN_QK + 2 * GDN_V + 4 * GDN_HEADS + 2 * D_MODEL
XA_HEADS = 4
XA_DH = D_MODEL // XA_HEADS
N_GROUPS = 4
EXP_PER_GROUP = 8
N_EXPERTS = N_GROUPS * EXP_PER_GROUP
TOP_K = 2
D_EXPERT = D_MODEL // 2
MOE_BLOCK = 128

kernel_name = "hybrid_gla_gdn_hmoe_encoder"


def rmsnorm(x, w):
    xf = x.astype(jnp.float32)
    y = xf * lax.rsqrt(jnp.mean(xf * xf, axis=-1, keepdims=True) + EPS)
    return (y * w.astype(jnp.float32)).astype(x.dtype)


def l2norm(x):
    return x * lax.rsqrt(jnp.sum(x * x, axis=-1, keepdims=True) + EPS)


def to_heads(t, n_heads):
    b, l, _ = t.shape
    return t.reshape(b, l, n_heads, -1).transpose(0, 2, 1, 3)


def from_heads(t):
    b, h, l, d = t.shape
    return t.transpose(0, 2, 1, 3).reshape(b, l, h * d)


def flip_seq(t):
    return jnp.flip(t, axis=2)


def to_chunks(t, c):
    b, h, l = t.shape[:3]
    return t.reshape((b, h, l // c, c) + t.shape[3:])


def centered_conv(x, w):
    kw = w.shape[0]
    pad = kw // 2
    l = x.shape[1]
    xp = jnp.pad(x, ((0, 0), (pad, pad), (0, 0)))
    y = xp[:, 0:l] * w[0]
    for i in range(1, kw):
        y = y + xp[:, i:i + l] * w[i]
    return y


def gla_chunked(q, k, v, gk):
    b, h, l, dk = q.shape
    dv = v.shape[-1]
    c = GLA_CHUNK
    q, k, v, gk = (to_chunks(t, c) for t in (q, k, v, gk))
    cum = jnp.cumsum(gk, axis=3)
    cum_last = cum[:, :, :, -1:, :]
    q_dec = q * jnp.exp(cum)
    k_inv = k * jnp.exp(-cum)
    k_end = k * jnp.exp(cum_last - cum)
    causal = jnp.tril(jnp.ones((c, c), dtype=bool))
    att = jnp.where(causal, jnp.einsum("bhnid,bhnjd->bhnij", q_dec, k_inv), 0.0)
    o_intra = jnp.einsum("bhnij,bhnjv->bhniv", att, v)

    def step(state, inp):
        qd, ke, vc, dec = inp
        o = jnp.einsum("bhid,bhdv->bhiv", qd, state)
        state = dec[..., None] * state + jnp.einsum("bhid,bhiv->bhdv", ke, vc)
        return state, o

    xs = tuple(jnp.moveaxis(t, 2, 0) for t in (q_dec, k_end, v, jnp.exp(cum_last[:, :, :, 0, :])))
    s0 = jnp.zeros((b, h, dk, dv), jnp.float32)
    _, o_inter = lax.scan(step, s0, xs)
    o = o_intra + jnp.moveaxis(o_inter, 0, 2)
    return o.reshape(b, h, l, dv)


def gdn_chunked(q, k, v, g, beta):
    b, h, l, dk = q.shape
    dv = v.shape[-1]
    c = GDN_CHUNK
    q, k, v, g, beta = (to_chunks(t, c) for t in (q, k, v, g, beta))
    cum = jnp.cumsum(g, axis=-1)
    idx = jnp.arange(c)
    incl = idx[:, None] >= idx[None, :]
    strict = idx[:, None] > idx[None, :]
    decay_ij = jnp.exp(jnp.where(incl, cum[..., :, None] - cum[..., None, :], -jnp.inf))
    kb = k * beta[..., None]
    a_strict = jnp.where(strict, jnp.einsum("bhnid,bhnjd->bhnij", kb, k) * decay_ij, 0.0)
    t_mat = a_strict + jnp.eye(c, dtype=a_strict.dtype)
    u = lax.linalg.triangular_solve(t_mat, v * beta[..., None], left_side=True, lower=True, unit_diagonal=True)
    w = lax.linalg.triangular_solve(t_mat, kb * jnp.exp(cum)[..., None], left_side=True, lower=True, unit_diagonal=True)
    att = jnp.einsum("bhnid,bhnjd->bhnij", q, k) * decay_ij
    q_dec = q * jnp.exp(cum)[..., None]
    k_end = k * jnp.exp(cum[..., -1:] - cum)[..., None]
    dec = jnp.exp(cum[..., -1])

    def step(state, inp):
        wc, uc, qd, ac, ke, dc = inp
        v_new = uc - jnp.einsum("bhid,bhdv->bhiv", wc, state)
        o = jnp.einsum("bhid,bhdv->bhiv", qd, state) + jnp.einsum("bhij,bhjv->bhiv", ac, v_new)
        state = dc[..., None, None] * state + jnp.einsum("bhid,bhiv->bhdv", ke, v_new)
        return state, o

    xs = tuple(jnp.moveaxis(t, 2, 0) for t in (w, u, q_dec, att, k_end, dec))
    s0 = jnp.zeros((b, h, dk, dv), jnp.float32)
    _, o = lax.scan(step, s0, xs)
    return jnp.moveaxis(o, 0, 2).reshape(b, h, l, dv)


def token_mixer(xn, p, l):
    dt = xn.dtype
    f32 = jnp.float32
    z = xn @ p["w_in"][l]
    (a_q, a_k, a_v, a_r, a_lr_f, a_lr_b,
     b_q, b_k, b_v, b_g, b_beta_f, b_beta_b, b_a_f, b_a_b,
     m_a, m_b) = jnp.split(z, np.cumsum(IN_SPLITS)[:-1].tolist(), axis=-1)

    q = to_heads(a_q, GLA_HEADS).astype(f32) * GLA_DK ** -0.5
    k = to_heads(a_k, GLA_HEADS).astype(f32)
    v = to_heads(a_v, GLA_HEADS).astype(f32)
    gk_f = to_heads(jax.nn.log_sigmoid((a_lr_f @ p["gla_w_up_f"][l] + p["gla_b_up_f"][l]).astype(f32)) / GLA_NORMALIZER, GLA_HEADS)
    gk_b = to_heads(jax.nn.log_sigmoid((a_lr_b @ p["gla_w_up_b"][l] + p["gla_b_up_b"][l]).astype(f32)) / GLA_NORMALIZER, GLA_HEADS)
    o_a = gla_chunked(q, k, v, gk_f) + flip_seq(gla_chunked(flip_seq(q), flip_seq(k), flip_seq(v), flip_seq(gk_b)))
    o_a = from_heads(rmsnorm(o_a, p["gla_onorm"][l])).astype(dt) * jax.nn.silu(a_r)

    qkv = jax.nn.silu(centered_conv(jnp.concatenate([b_q, b_k, b_v], axis=-1), p["gdn_conv"][l]))
    cq, ck, cv = jnp.split(qkv, [GDN_QK, 2 * GDN_QK], axis=-1)
    q = l2norm(to_heads(cq, GDN_HEADS).astype(f32)) * GDN_DK ** -0.5
    k = l2norm(to_heads(ck, GDN_HEADS).astype(f32))
    v = to_heads(cv, GDN_HEADS).astype(f32)

    def gates(beta_logit, a_logit, a_log, dt_bias):
        beta = jax.nn.sigmoid(beta_logit.astype(f32)).transpose(0, 2, 1)
        g = (-jnp.exp(a_log.astype(f32)) * jax.nn.softplus(a_logit.astype(f32) + dt_bias.astype(f32))).transpose(0, 2, 1)
        return g, beta

    g_f, beta_f = gates(b_beta_f, b_a_f, p["gdn_a_log_f"][l], p["gdn_dt_bias_f"][l])
    g_b, beta_b = gates(b_beta_b, b_a_b, p["gdn_a_log_b"][l], p["gdn_dt_bias_b"][l])
    o_b = gdn_chunked(q, k, v, g_f, beta_f) + flip_seq(gdn_chunked(flip_seq(q), flip_seq(k), flip_seq(v), flip_seq(g_b), flip_seq(beta_b)))
    o_b = from_heads(rmsnorm(o_b, p["gdn_onorm"][l])).astype(dt) * jax.nn.silu(b_g)

    y = jax.nn.sigmoid(m_a) * o_a + jax.nn.sigmoid(m_b) * o_b
    return y @ p["w_out"][l]


def memory_cross_attn(hn, mn, wq, wk, wv, wo):
    b, l, d = hn.shape
    m = mn.shape[1]
    q = (hn @ wq).reshape(b, l, XA_HEADS, XA_DH)
    k = (mn @ wk).reshape(b, m, XA_HEADS, XA_DH)
    v = (mn @ wv).reshape(b, m, XA_HEADS, XA_DH)
    s = jnp.einsum("blhd,bmhd->bhlm", q, k).astype(jnp.float32) * XA_DH ** -0.5
    a = jax.nn.softmax(s, axis=-1).astype(v.dtype)
    o = jnp.einsum("bhlm,bmhd->blhd", a, v).reshape(b, l, d)
    return o @ wo


def grouped_expert_ffn(xt, expert_id, weight, w_gate, w_up, w_down):
    n, d = xt.shape
    m = n * TOP_K
    blk = MOE_BLOCK
    eid = expert_id.reshape(m)
    tok = jnp.repeat(jnp.arange(n, dtype=jnp.int32), TOP_K)
    wt = weight.reshape(m)
    order = jnp.argsort(eid)
    e_sorted = eid[order]
    counts = jnp.bincount(eid, length=N_EXPERTS)
    padded = (counts + blk - 1) // blk * blk
    starts = jnp.cumsum(counts) - counts
    pends = jnp.cumsum(padded)
    pstarts = pends - padded
    dest = pstarts[e_sorted] + (jnp.arange(m) - starts[e_sorted])
    n_blocks = -(-m // blk) + N_EXPERTS
    p_len = n_blocks * blk
    slot_tok = jnp.full((p_len,), n, jnp.int32).at[dest].set(tok[order])
    slot_w = jnp.zeros((p_len,), jnp.float32).at[dest].set(wt[order])
    block_e = jnp.minimum(jnp.searchsorted(pends, jnp.arange(n_blocks) * blk, side="right"), N_EXPERTS - 1)
    x_pad = jnp.concatenate([xt, jnp.zeros((1, d), xt.dtype)], axis=0)

    def block_ffn(args):
        toks, e = args
        xb = x_pad[toks]
        hb = jax.nn.silu(xb @ w_gate[e]) * (xb @ w_up[e])
        return hb @ w_down[e]

    out = lax.map(block_ffn, (slot_tok.reshape(n_blocks, blk), block_e)).reshape(p_len, d)
    out = out * slot_w[:, None].astype(out.dtype)
    return jnp.zeros((n + 1, d), out.dtype).at[slot_tok].add(out)[:n]


def hier_moe(xn, w_group, b_group, w_expert, b_expert, w_gate, w_up, w_down):
    b, l, d = xn.shape
    n = b * l
    xt = xn.reshape(n, d)
    grp_logits = (xt @ w_group).astype(jnp.float32) + b_group.astype(jnp.float32)
    grp_prob = jax.nn.softmax(grp_logits, axis=-1)
    g_idx = jnp.argmax(grp_logits, axis=-1)
    g_w = jnp.take_along_axis(grp_prob, g_idx[:, None], axis=-1)[:, 0]
    exp_logits = ((xt @ w_expert).astype(jnp.float32) + b_expert.astype(jnp.float32)).reshape(n, N_GROUPS, EXP_PER_GROUP)
    sel_logits = jnp.take_along_axis(exp_logits, g_idx[:, None, None], axis=1)[:, 0]
    sel_prob = jax.nn.softmax(sel_logits, axis=-1)
    top_p, top_i = lax.top_k(sel_prob, TOP_K)
    top_p = top_p / jnp.sum(top_p, axis=-1, keepdims=True)
    expert_id = (g_idx[:, None] * EXP_PER_GROUP + top_i).astype(jnp.int32)
    weight = g_w[:, None] * top_p
    y = grouped_expert_ffn(xt, expert_id, weight, w_gate, w_up, w_down)
    return y.reshape(b, l, d)


def encoder_layer(h, mem, p, l):
    h = h + token_mixer(rmsnorm(h, p["ln_mix"][l]), p, l)
    h = h + memory_cross_attn(rmsnorm(h, p["ln_xa"][l]), rmsnorm(mem, p["ln_mem"][l]),
                              p["xa_wq"][l], p["xa_wk"][l], p["xa_wv"][l], p["xa_wo"][l])
    h = h + hier_moe(rmsnorm(h, p["ln_moe"][l]), p["moe_w_group"][l], p["moe_b_group"][l],
                     p["moe_w_expert"][l], p["moe_b_expert"][l],
                     p["moe_w_gate"][l], p["moe_w_up"][l], p["moe_w_down"][l])
    return h


def encode(x, mem, p):
    h = x
    for l in range(DEPTH):
        h = encoder_layer(h, mem, p, l)
    return rmsnorm(h, p["ln_final"])


def setup_inputs(seed: int = 0) -> dict:
    key = jax.random.key(seed)
    ks = iter(jax.random.split(key, 40))
    f32 = jnp.float32

    def nrm(shape, scale):
        return jax.random.normal(next(ks), shape, f32) * scale

    def gain(shape):
        return 1.0 + 0.02 * jax.random.normal(next(ks), shape, f32)

    def a_log():
        return jnp.log(jax.random.uniform(next(ks), (DEPTH, GDN_HEADS), f32, minval=1.0, maxval=16.0))

    def dt_bias():
        dtv = jnp.exp(jax.random.uniform(next(ks), (DEPTH, GDN_HEADS), f32, minval=math.log(1e-3), maxval=math.log(0.1)))
        return dtv + jnp.log(-jnp.expm1(-dtv))

    dsc = D_MODEL ** -0.5
    return {
        "x_prompt": nrm((BATCH, SEQ, D_MODEL), 1.0),
        "x_sample": nrm((DEC_BATCH, DEC_SEQ, D_MODEL), 1.0),
        "mem_prompt": nrm((BATCH, N_MEM, D_MODEL), 1.0),
        "mem_sample": nrm((DEC_BATCH, N_MEM, D_MODEL), 1.0),
        "ln_mix": gain((DEPTH, D_MODEL)),
        "w_in": nrm((DEPTH, D_MODEL, IN_WIDTH), dsc),
        "gla_w_up_f": nrm((DEPTH, GLA_RANK, GLA_QK), GLA_RANK ** -0.5),
        "gla_b_up_f": nrm((DEPTH, GLA_QK), 0.1),
        "gla_w_up_b": nrm((DEPTH, GLA_RANK, GLA_QK), GLA_RANK ** -0.5),
        "gla_b_up_b": nrm((DEPTH, GLA_QK), 0.1),
        "gla_onorm": gain((DEPTH, GLA_DV)),
        "gdn_conv": nrm((DEPTH, GDN_CONV, 2 * GDN_QK + GDN_V), GDN_CONV ** -0.5),
        "gdn_a_log_f": a_log(),
        "gdn_dt_bias_f": dt_bias(),
        "gdn_a_log_b": a_log(),
        "gdn_dt_bias_b": dt_bias(),
        "gdn_onorm": gain((DEPTH, GDN_DV)),
        "w_out": nrm((DEPTH, D_MODEL, D_MODEL), dsc),
        "ln_xa": gain((DEPTH, D_MODEL)),
        "ln_mem": gain((DEPTH, D_MODEL)),
        "xa_wq": nrm((DEPTH, D_MODEL, D_MODEL), dsc),
        "xa_wk": nrm((DEPTH, D_MODEL, D_MODEL), dsc),
        "xa_wv": nrm((DEPTH, D_MODEL, D_MODEL), dsc),
        "xa_wo": nrm((DEPTH, D_MODEL, D_MODEL), dsc),
        "ln_moe": gain((DEPTH, D_MODEL)),
        "moe_w_group": nrm((DEPTH, D_MODEL, N_GROUPS), dsc),
        "moe_b_group": nrm((DEPTH, N_GROUPS), 0.01),
        "moe_w_expert": nrm((DEPTH, D_MODEL, N_EXPERTS), dsc),
        "moe_b_expert": nrm((DEPTH, N_EXPERTS), 0.01),
        "moe_w_gate": nrm((DEPTH, N_EXPERTS, D_MODEL, D_EXPERT), dsc),
        "moe_w_up": nrm((DEPTH, N_EXPERTS, D_MODEL, D_EXPERT), dsc),
        "moe_w_down": nrm((DEPTH, N_EXPERTS, D_EXPERT, D_MODEL), D_EXPERT ** -0.5),
        "ln_final": gain((D_MODEL,)),
    }


def reference(x_prompt, x_sample, mem_prompt, mem_sample, ln_mix, w_in,
              gla_w_up_f, gla_b_up_f, gla_w_up_b, gla_b_up_b, gla_onorm,
              gdn_conv, gdn_a_log_f, gdn_dt_bias_f, gdn_a_log_b, gdn_dt_bias_b, gdn_onorm,
              w_out, ln_xa, ln_mem, xa_wq, xa_wk, xa_wv, xa_wo,
              ln_moe, moe_w_group, moe_b_group, moe_w_expert, moe_b_expert,
              moe_w_gate, moe_w_up, moe_w_down, ln_final):
    p = dict(ln_mix=ln_mix, w_in=w_in,
             gla_w_up_f=gla_w_up_f, gla_b_up_f=gla_b_up_f, gla_w_up_b=gla_w_up_b, gla_b_up_b=gla_b_up_b,
             gla_onorm=gla_onorm, gdn_conv=gdn_conv,
             gdn_a_log_f=gdn_a_log_f, gdn_dt_bias_f=gdn_dt_bias_f,
             gdn_a_log_b=gdn_a_log_b, gdn_dt_bias_b=gdn_dt_bias_b, gdn_onorm=gdn_onorm,
             w_out=w_out, ln_xa=ln_xa, ln_mem=ln_mem,
             xa_wq=xa_wq, xa_wk=xa_wk, xa_wv=xa_wv, xa_wo=xa_wo,
             ln_moe=ln_moe, moe_w_group=moe_w_group, moe_b_group=moe_b_group,
             moe_w_expert=moe_w_expert, moe_b_expert=moe_b_expert,
             moe_w_gate=moe_w_gate, moe_w_up=moe_w_up, moe_w_down=moe_w_down,
             ln_final=ln_final)
    y_prompt = encode(x_prompt, mem_prompt, p)
    y_sample = encode(x_sample, mem_sample, p)
    return (y_prompt, y_sample)
```

```python
import functools
import math

import jax
import jax.numpy as jnp
import numpy as np
from jax import lax
from jax.experimental import pallas as pl
from jax.experimental.pallas import tpu as pltpu

F32 = jnp.float32
BF16 = jnp.bfloat16
I32 = jnp.int32

D_MODEL = 1024
EPS = 1e-6
N_HEADS = 4
DK = 128
DV = 256
QK = N_HEADS * DK
GLA_RANK = 16
GLA_NORMALIZER = 16.0
CHUNK = 64
XA_DH = D_MODEL // N_HEADS
N_GROUPS = 4
EXP_PER_GROUP = 8
N_EXPERTS = N_GROUPS * EXP_PER_GROUP
D_EXPERT = D_MODEL // 2
LANES = 128

VMEM_LIMIT = 56 * 1024 * 1024


def _cparams(sem):
    return pltpu.CompilerParams(dimension_semantics=sem, vmem_limit_bytes=VMEM_LIMIT)


def _dot(a, b):
    return jnp.dot(a, b, preferred_element_type=F32)


def _dot_nt(a, b):
    return lax.dot_general(a, b, (((1,), (1,)), ((), ())), preferred_element_type=F32)


def _dot_tn(a, b):
    return lax.dot_general(a, b, (((0,), (0,)), ((), ())), preferred_element_type=F32)


def _split3(x):
    hi = x.astype(BF16)
    r = x - hi.astype(F32)
    mid = r.astype(BF16)
    lo = (r - mid.astype(F32)).astype(BF16)
    return hi, mid, lo


def _exact_lhs_dot(m_bf16, x):
    hi, mid, lo = _split3(x)
    return _dot(m_bf16, hi) + _dot(m_bf16, mid) + _dot(m_bf16, lo)


def _dot3(a, b):
    a_hi = a.astype(BF16)
    a_lo = (a - a_hi.astype(F32)).astype(BF16)
    b_hi = b.astype(BF16)
    b_lo = (b - b_hi.astype(F32)).astype(BF16)
    return _dot(a_hi, b_hi) + _dot(a_hi, b_lo) + _dot(a_lo, b_hi)


def _sigmoid(x):
    return 1.0 / (1.0 + jnp.exp(-x))


def _silu(x):
    return x * _sigmoid(x)


def _softplus(x):
    return jnp.maximum(x, 0.0) + jnp.log(1.0 + jnp.exp(-jnp.abs(x)))


def _log_sigmoid(x):
    return jnp.minimum(x, 0.0) - jnp.log(1.0 + jnp.exp(-jnp.abs(x)))


_MAIN_WIDTHS = (QK, QK, D_MODEL, D_MODEL, 2 * QK + D_MODEL, D_MODEL, D_MODEL, D_MODEL)
_MAIN_TOTAL = sum(_MAIN_WIDTHS)
_COL_TILE = 512


def _inproj_body(x_ref, g_ref, w_ref, wl_ref, wg_ref, aq, ak, av, ar, bqkv, bg, ma, mb, lr, gates):
    x = x_ref[...]
    ms = jnp.mean(x * x, axis=-1, keepdims=True)
    xn = (x * lax.rsqrt(ms + EPS) * g_ref[...]).astype(BF16)
    col = 0
    for ref in (aq, ak, av, ar, bqkv, bg, ma, mb):
        width = ref.shape[-1]
        for c in range(0, width, _COL_TILE):
            ref[:, c:c + _COL_TILE] = _dot(xn, w_ref[:, col + c:col + c + _COL_TILE]).astype(ref.dtype)
        col += width
    lr[...] = _dot(xn, wl_ref[...])
    gates[...] = _dot(xn, wg_ref[...])


def _pack_in_weights(w_in):
    o = np.cumsum((0, QK, QK, D_MODEL, D_MODEL, GLA_RANK, GLA_RANK,
                   QK, QK, D_MODEL, D_MODEL, N_HEADS, N_HEADS, N_HEADS, N_HEADS, D_MODEL, D_MODEL))
    seg = lambda i: w_in[:, o[i]:o[i + 1]]
    w_main = jnp.concatenate([seg(0), seg(1), seg(2), seg(3), seg(6), seg(7), seg(8), seg(9),
                              seg(14), seg(15)], axis=1).astype(BF16)
    w_lr = jnp.concatenate([seg(4), seg(5), jnp.zeros((D_MODEL, LANES - 2 * GLA_RANK), F32)], axis=1).astype(BF16)
    g4 = jnp.stack([seg(10), seg(11), seg(12), seg(13)], axis=-1)
    w_g = jnp.concatenate([g4, jnp.zeros((D_MODEL, N_HEADS, LANES - 4), F32)], axis=-1)
    w_g = w_g.reshape(D_MODEL, N_HEADS * LANES).astype(BF16)
    return w_main, w_lr, w_g


def _in_proj(x2d, ln_mix, w_main, w_lr, w_g, tm):
    n = x2d.shape[0]
    widths = _MAIN_WIDTHS
    const = lambda i: (0, 0)
    row = lambda i: (i, 0)
    out_shape = [jax.ShapeDtypeStruct((n, w), BF16) for w in widths]
    out_shape += [jax.ShapeDtypeStruct((n, LANES), F32), jax.ShapeDtypeStruct((n, N_HEADS * LANES), F32)]
    out_specs = [pl.BlockSpec((tm, w), row) for w in widths]
    out_specs += [pl.BlockSpec((tm, LANES), row), pl.BlockSpec((tm, N_HEADS * LANES), row)]
    single = pl.Buffered(1)
    return pl.pallas_call(
        _inproj_body,
        grid=(n // tm,),
        in_specs=[pl.BlockSpec((tm, D_MODEL), row),
                  pl.BlockSpec((1, D_MODEL), const),
                  pl.BlockSpec((D_MODEL, _MAIN_TOTAL), const, pipeline_mode=single),
                  pl.BlockSpec((D_MODEL, LANES), const, pipeline_mode=single),
                  pl.BlockSpec((D_MODEL, N_HEADS * LANES), const, pipeline_mode=single)],
        out_specs=out_specs,
        out_shape=out_shape,
        compiler_params=_cparams(("parallel",)),
        name="in_proj",
    )(x2d, ln_mix.reshape(1, D_MODEL), w_main, w_lr, w_g)


def _tri_masks(c):
    r = lax.broadcasted_iota(I32, (c, c), 0)
    s = lax.broadcasted_iota(I32, (c, c), 1)
    return r, s


def _gla_stream(q_ref, k_ref, v_ref, lr_ref, w_ref, b_ref, o_ref, s_ref, reverse):
    tb = q_ref.shape[0]
    c = CHUNK
    nc = tb // c
    logits = _dot(lr_ref[...].astype(BF16), w_ref[...]) + b_ref[...]
    gk = _log_sigmoid(logits) * (1.0 / GLA_NORMALIZER)
    r, s = _tri_masks(c)
    incl = (r <= s) if reverse else (r >= s)
    tri = jnp.where(incl, 1.0, 0.0).astype(BF16)
    order = range(nc - 1, -1, -1) if reverse else range(nc)
    for ci in order:
        r0 = ci * c
        cum = _exact_lhs_dot(tri, gk[r0:r0 + c])
        tot = cum[0:1] if reverse else cum[c - 1:c]
        q = q_ref[r0:r0 + c, :].astype(F32)
        k = k_ref[r0:r0 + c, :].astype(F32)
        v = v_ref[r0:r0 + c, :]
        q_dec = (q * jnp.exp(cum) * (DK ** -0.5)).astype(BF16)
        k_inv = (k * jnp.exp(-cum)).astype(BF16)
        k_end = (k * jnp.exp(tot - cum)).astype(BF16)
        att = jnp.where(incl, _dot_nt(q_dec, k_inv), 0.0).astype(BF16)
        state = s_ref[...]
        o = _dot(att, v) + _dot(q_dec, state.astype(BF16))
        o_ref[r0:r0 + c, :] = o.astype(o_ref.dtype)
        dec = jnp.transpose(jnp.broadcast_to(jnp.exp(tot), (DK, DK)))
        upd = _dot_tn(k_end, v)
        s_ref[:, 0:DK] = state[:, 0:DK] * dec + upd[:, 0:DK]
        s_ref[:, DK:DV] = state[:, DK:DV] * dec + upd[:, DK:DV]


def _gla_body(qf, kf, vf, lrf, qb, kb, vb, lrb, wf, bf, wb, bb, of, ob, sf, sb):
    @pl.when(pl.program_id(2) == 0)
    def _():
        sf[...] = jnp.zeros_like(sf)
        sb[...] = jnp.zeros_like(sb)

    _gla_stream(qf, kf, vf, lrf, wf, bf, of, sf, reverse=False)
    _gla_stream(qb, kb, vb, lrb, wb, bb, ob, sb, reverse=True)


def _gla(aq, ak, av, lr, w_up, b_up, tb):
    bsz, l, _ = aq.shape
    nb = l // tb
    fwd = lambda b, h, i: (b, i, h)
    bwd = lambda b, h, i: (b, nb - 1 - i, h)
    fwd0 = lambda b, h, i: (b, i, 0)
    bwd0 = lambda b, h, i: (b, nb - 1 - i, 0)
    qk_spec = lambda m: pl.BlockSpec((None, tb, DK), m)
    v_spec = lambda m: pl.BlockSpec((None, tb, DV), m)
    lr_spec = lambda m: pl.BlockSpec((None, tb, LANES), m)
    in_specs = [qk_spec(fwd), qk_spec(fwd), v_spec(fwd), lr_spec(fwd0),
                qk_spec(bwd), qk_spec(bwd), v_spec(bwd), lr_spec(bwd0),
                pl.BlockSpec((LANES, DK), lambda b, h, i: (0, h)),
                pl.BlockSpec((1, DK), lambda b, h, i: (0, h)),
                pl.BlockSpec((LANES, DK), lambda b, h, i: (0, N_HEADS + h)),
                pl.BlockSpec((1, DK), lambda b, h, i: (0, N_HEADS + h))]
    out = jax.ShapeDtypeStruct((bsz, l, D_MODEL), BF16)
    return pl.pallas_call(
        _gla_body,
        grid=(bsz, N_HEADS, nb),
        in_specs=in_specs,
        out_specs=[v_spec(fwd), v_spec(bwd)],
        out_shape=[out, out],
        scratch_shapes=[pltpu.VMEM((DK, DV), F32), pltpu.VMEM((DK, DV), F32)],
        compiler_params=_cparams(("parallel", "parallel", "arbitrary")),
        name="gla",
    )(aq, ak, av, lr, aq, ak, av, lr, w_up, b_up, w_up, b_up)


def _pack_gla_up(w_up_f, b_up_f, w_up_b, b_up_b):
    w = jnp.zeros((LANES, 2 * QK), F32)
    w = w.at[0:GLA_RANK, 0:QK].set(w_up_f).at[GLA_RANK:2 * GLA_RANK, QK:2 * QK].set(w_up_b)
    b = jnp.concatenate([b_up_f, b_up_b]).reshape(1, 2 * QK).astype(F32)
    return w.astype(BF16), b


_HALO = 16


def _gdn_prep_body(x_ref, prev_ref, next_ref, w_ref, q_ref, k_ref, v_ref):
    i = pl.program_id(1)
    nb = pl.num_programs(1)
    tb = x_ref.shape[0]
    x = x_ref[...].astype(F32)
    prev_row = jnp.where(i > 0, prev_ref[_HALO - 1:_HALO, :].astype(F32), 0.0)
    next_row = jnp.where(i < nb - 1, next_ref[0:1, :].astype(F32), 0.0)
    ridx = lax.broadcasted_iota(I32, (tb, 1), 0)
    x_prev = jnp.where(ridx == 0, prev_row, pltpu.roll(x, 1, axis=0))
    x_next = jnp.where(ridx == tb - 1, next_row, pltpu.roll(x, tb - 1, axis=0))
    y = _silu(x_prev * w_ref[0:1, :] + x * w_ref[1:2, :] + x_next * w_ref[2:3, :])
    for h in range(N_HEADS):
        qh = y[:, h * DK:(h + 1) * DK]
        q_ref[:, h * DK:(h + 1) * DK] = (
            qh * (lax.rsqrt(jnp.sum(qh * qh, axis=-1, keepdims=True) + EPS) * DK ** -0.5)).astype(q_ref.dtype)
        kh = y[:, QK + h * DK:QK + (h + 1) * DK]
        k_ref[:, h * DK:(h + 1) * DK] = (
            kh * lax.rsqrt(jnp.sum(kh * kh, axis=-1, keepdims=True) + EPS)).astype(k_ref.dtype)
    v_ref[...] = y[:, 2 * QK:].astype(v_ref.dtype)


def _gdn_prep(bqkv, conv_w, tb):
    bsz, l, width = bqkv.shape
    nb = l // tb
    hb = tb // _HALO
    nh = l // _HALO
    return pl.pallas_call(
        _gdn_prep_body,
        grid=(bsz, nb),
        in_specs=[pl.BlockSpec((None, tb, width), lambda b, i: (b, i, 0)),
                  pl.BlockSpec((None, _HALO, width), lambda b, i: (b, jnp.maximum(i * hb - 1, 0), 0)),
                  pl.BlockSpec((None, _HALO, width), lambda b, i: (b, jnp.minimum((i + 1) * hb, nh - 1), 0)),
                  pl.BlockSpec((3, width), lambda b, i: (0, 0))],
        out_specs=[pl.BlockSpec((None, tb, QK), lambda b, i: (b, i, 0)),
                   pl.BlockSpec((None, tb, QK), lambda b, i: (b, i, 0)),
                   pl.BlockSpec((None, tb, D_MODEL), lambda b, i: (b, i, 0))],
        out_shape=[jax.ShapeDtypeStruct((bsz, l, QK), BF16), jax.ShapeDtypeStruct((bsz, l, QK), BF16),
                   jax.ShapeDtypeStruct((bsz, l, D_MODEL), BF16)],
        compiler_params=_cparams(("parallel", "parallel")),
        name="gdn_prep",
    )(bqkv, bqkv, bqkv, conv_w.astype(F32))


_G_BETA = 0
_G_A = 2


def _pack_gdn_params(a_log_f, dt_bias_f, a_log_b, dt_bias_b):
    p = jnp.zeros((8, N_HEADS, LANES), F32)
    p = p.at[0, :, _G_A].set(-jnp.exp(a_log_f.astype(F32))).at[0, :, _G_A + 1].set(-jnp.exp(a_log_b.astype(F32)))
    p = p.at[1, :, _G_A].set(dt_bias_f.astype(F32)).at[1, :, _G_A + 1].set(dt_bias_b.astype(F32))
    return p.reshape(8, N_HEADS * LANES)


def _unit_tri_inverse(a, r, s, eye):
    same16 = (r >> 4) == (s >> 4)
    same32 = (r >> 5) == (s >> 5)
    d = jnp.where(same16, a, 0.0)
    n = eye - d
    p = _dot3(d, d)
    n = n + _dot3(n, p)
    p = _dot3(p, p)
    n = n + _dot3(n, p)
    p = _dot3(p, p)
    n = n + _dot3(n, p)
    e1 = jnp.where(jnp.logical_and(same32, jnp.logical_not(same16)), a, 0.0)
    n = n - _dot3(_dot3(n, e1), n)
    e2 = jnp.where(same32, 0.0, a)
    n = n - _dot3(_dot3(n, e2), n)
    return n


def _gdn_stream(q_ref, k_ref, v_ref, g_ref, p_ref, tri_ref, o_ref, s_ref, reverse):
    tb = q_ref.shape[0]
    c = CHUNK
    nc = tb // c
    d = 1 if reverse else 0
    gates = g_ref[...]
    g_all = p_ref[0:1, :] * _softplus(gates + p_ref[1:2, :])
    beta_all = _sigmoid(gates)
    cum_all = _exact_lhs_dot(tri_ref[...], g_all)
    cum_t = jnp.transpose(cum_all)
    r, s = _tri_masks(c)
    incl = (r <= s) if reverse else (r >= s)
    strict = (r < s) if reverse else (r > s)
    eye = jnp.where(r == s, 1.0, 0.0)
    order = range(nc - 1, -1, -1) if reverse else range(nc)
    la = _G_A + d
    lb = _G_BETA + d
    for ci in order:
        r0 = ci * c
        ccol = cum_all[r0:r0 + c, la:la + 1]
        crow = cum_t[la:la + 1, r0:r0 + c]
        tot = cum_all[r0:r0 + 1, la:la + 1] if reverse else cum_all[r0 + c - 1:r0 + c, la:la + 1]
        beta = beta_all[r0:r0 + c, lb:lb + 1]
        decay = jnp.where(incl, jnp.exp(jnp.where(incl, ccol - crow, 0.0)), 0.0)
        qb = q_ref[r0:r0 + c, :]
        kb = k_ref[r0:r0 + c, :]
        q = qb.astype(F32)
        k = kb.astype(F32)
        v = v_ref[r0:r0 + c, :].astype(F32)
        kk = _dot_nt(kb, kb)
        a = jnp.where(strict, kk * beta * decay, 0.0)
        t_inv = _unit_tri_inverse(a, r, s, eye)
        e_cum = jnp.exp(ccol)
        t_hi = t_inv.astype(BF16)
        t_lo = (t_inv - t_hi.astype(F32)).astype(BF16)
        vb = (v * beta).astype(BF16)
        kbd = (k * (beta * e_cum)).astype(BF16)
        u = _dot(t_hi, vb) + _dot(t_lo, vb)
        w = (_dot(t_hi, kbd) + _dot(t_lo, kbd)).astype(BF16)
        att = (_dot_nt(qb, kb) * decay).astype(BF16)
        q_dec = (q * e_cum).astype(BF16)
        k_end = (k * jnp.exp(tot - ccol)).astype(BF16)
        state = s_ref[...]
        sb = state.astype(BF16)
        v_new = (u - _dot(w, sb)).astype(BF16)
        o = _dot(q_dec, sb) + _dot(att, v_new)
        o_ref[r0:r0 + c, :] = o.astype(o_ref.dtype)
        s_ref[...] = state * jnp.exp(tot) + _dot_tn(k_end, v_new)


def _gdn_body(qf, kf, vf, gf, qb, kb, vb, gb, p_ref, trif, trib, of, ob, sf, sb):
    @pl.when(pl.program_id(2) == 0)
    def _():
        sf[...] = jnp.zeros_like(sf)
        sb[...] = jnp.zeros_like(sb)

    _gdn_stream(qf, kf, vf, gf, p_ref, trif, of, sf, reverse=False)
    _gdn_stream(qb, kb, vb, gb, p_ref, trib, ob, sb, reverse=True)


def _chunk_tri(tb, reverse):
    i = np.arange(tb)
    same = (i[:, None] // CHUNK) == (i[None, :] // CHUNK)
    tri = (i[:, None] <= i[None, :]) if reverse else (i[:, None] >= i[None, :])
    return jnp.asarray(np.where(same & tri, 1.0, 0.0), BF16)


def _gdn(cq, ck, cv, gates, gparams, tb):
    bsz, l, _ = cq.shape
    nb = l // tb
    fwd = lambda b, h, i: (b, i, h)
    bwd = lambda b, h, i: (b, nb - 1 - i, h)
    qk_spec = lambda m: pl.BlockSpec((None, tb, DK), m)
    v_spec = lambda m: pl.BlockSpec((None, tb, DV), m)
    g_spec = lambda m: pl.BlockSpec((None, tb, LANES), m)
    const = lambda b, h, i: (0, 0)
    in_specs = [qk_spec(fwd), qk_spec(fwd), v_spec(fwd), g_spec(fwd),
                qk_spec(bwd), qk_spec(bwd), v_spec(bwd), g_spec(bwd),
                pl.BlockSpec((8, LANES), lambda b, h, i: (0, h)),
                pl.BlockSpec((tb, tb), const), pl.BlockSpec((tb, tb), const)]
    out = jax.ShapeDtypeStruct((bsz, l, D_MODEL), BF16)
    return pl.pallas_call(
        _gdn_body,
        grid=(bsz, N_HEADS, nb),
        in_specs=in_specs,
        out_specs=[v_spec(fwd), v_spec(bwd)],
        out_shape=[out, out],
        scratch_shapes=[pltpu.VMEM((DK, DV), F32), pltpu.VMEM((DK, DV), F32)],
        compiler_params=_cparams(("parallel", "parallel", "arbitrary")),
        name="gdn",
    )(cq, ck, cv, gates, cq, ck, cv, gates, gparams, _chunk_tri(tb, False), _chunk_tri(tb, True))


def _rmsnorm(x, w):
    return x * lax.rsqrt(jnp.mean(x * x, axis=-1, keepdims=True) + EPS) * w


def _mem_kv_body(m_ref, g_ref, wk_ref, wv_ref, k_ref, v_ref):
    mn = _rmsnorm(m_ref[...], g_ref[...]).astype(BF16)
    k_ref[...] = _dot(mn, wk_ref[...]).astype(k_ref.dtype)
    v_ref[...] = _dot(mn, wv_ref[...]).astype(v_ref.dtype)


def _mem_kv(mem, ln_mem, wk, wv):
    bsz, m, _ = mem.shape
    const = lambda b: (0, 0)
    blk = pl.BlockSpec((None, m, D_MODEL), lambda b: (b, 0, 0))
    out = jax.ShapeDtypeStruct((bsz, m, D_MODEL), BF16)
    return pl.pallas_call(
        _mem_kv_body,
        grid=(bsz,),
        in_specs=[blk, pl.BlockSpec((1, D_MODEL), const),
                  pl.BlockSpec((D_MODEL, D_MODEL), const), pl.BlockSpec((D_MODEL, D_MODEL), const)],
        out_specs=[blk, blk],
        out_shape=[out, out],
        compiler_params=_cparams(("parallel",)),
        name="mem_kv",
    )(mem, ln_mem.reshape(1, D_MODEL), wk, wv)


_R_E1, _R_E2, _R_W1, _R_W2 = 0, 1, 2, 3


def _head_rmsnorm(o, w):
    parts = []
    for h in range(N_HEADS):
        oh = o[:, h * DV:(h + 1) * DV]
        parts.append(oh * lax.rsqrt(jnp.mean(oh * oh, axis=-1, keepdims=True) + EPS))
    return jnp.concatenate(parts, axis=-1) * w


def _route_tile(lg):
    neg = -1e30
    big = 1e9
    lane = lax.broadcasted_iota(I32, lg.shape, 1).astype(F32)
    gmask = lane < N_GROUPS
    gl = jnp.where(gmask, lg, neg)
    gmax = jnp.max(gl, axis=-1, keepdims=True)
    gidx = jnp.min(jnp.where(gl == gmax, lane, big), axis=-1, keepdims=True)
    gsum = jnp.sum(jnp.where(gmask, jnp.exp(gl - gmax), 0.0), axis=-1, keepdims=True)
    lo = N_GROUPS + EXP_PER_GROUP * gidx
    off = lane - lo
    el = jnp.where(jnp.abs(off - (EXP_PER_GROUP - 1) * 0.5) < EXP_PER_GROUP * 0.5, lg, neg)
    m1 = jnp.max(el, axis=-1, keepdims=True)
    i1 = jnp.min(jnp.where(el == m1, lane, big), axis=-1, keepdims=True)
    el2 = jnp.where(lane == i1, neg, el)
    m2 = jnp.max(el2, axis=-1, keepdims=True)
    i2 = jnp.min(jnp.where(el2 == m2, lane, big), axis=-1, keepdims=True)
    r = jnp.exp(m2 - m1)
    p1 = 1.0 / (1.0 + r)
    gw = 1.0 / gsum
    w1 = gw * p1
    w2 = gw * (r * p1)
    e1 = i1 - N_GROUPS
    e2 = i2 - N_GROUPS
    return jnp.where(lane == _R_E1, e1, jnp.where(lane == _R_E2, e2,
                     jnp.where(lane == _R_W1, w1, jnp.where(lane == _R_W2, w2, 0.0))))


def _post_body(x_ref, oaf, oab, obf, obb, ar, bg, ma, mb, na_ref, nb_ref, wout, lnxa, wq, kmem, vmem, wo, lnmoe,
               wrh, wrl, br, h2_ref, hn2_ref, route_ref):
    f = lambda ref: ref[...].astype(F32)
    oa = _head_rmsnorm(f(oaf) + f(oab), na_ref[...]) * _silu(f(ar))
    ob = _head_rmsnorm(f(obf) + f(obb), nb_ref[...]) * _silu(f(bg))
    y = _sigmoid(f(ma)) * oa + _sigmoid(f(mb)) * ob
    h1 = x_ref[...] + _dot(y.astype(BF16), wout[...])
    hn = _rmsnorm(h1, lnxa[...]).astype(BF16)
    q = _dot(hn, wq[...])
    outs = []
    for h in range(N_HEADS):
        sl = slice(h * XA_DH, (h + 1) * XA_DH)
        s = _dot_nt(q[:, sl].astype(BF16), kmem[:, sl]) * (XA_DH ** -0.5)
        p = jnp.exp(s - jnp.max(s, axis=-1, keepdims=True))
        denom = jnp.sum(p, axis=-1, keepdims=True)
        outs.append(_dot(p.astype(BF16), vmem[:, sl]) * (1.0 / denom))
    o = jnp.concatenate(outs, axis=-1)
    h2 = h1 + _dot(o.astype(BF16), wo[...])
    h2_ref[...] = h2
    hn2 = _rmsnorm(h2, lnmoe[...])
    hn2_ref[...] = hn2
    x_hi = hn2.astype(BF16)
    x_lo = (hn2 - x_hi.astype(F32)).astype(BF16)
    lg = _dot(x_hi, wrh[...]) + _dot(x_hi, wrl[...]) + _dot(x_lo, wrh[...]) + br[...]
    route_ref[...] = _route_tile(lg)


def _post(x, oaf, oab, obf, obb, ar, bg, ma, mb, kmem, vmem, p, tm):
    bsz, l, _ = x.shape
    nb = l // tm
    blk = pl.BlockSpec((None, tm, D_MODEL), lambda b, i: (b, i, 0))
    const = lambda b, i: (0, 0)
    vec = pl.BlockSpec((1, D_MODEL), const)
    mat = pl.BlockSpec((D_MODEL, D_MODEL), const)
    memb = pl.BlockSpec((None, kmem.shape[1], D_MODEL), lambda b, i: (b, 0, 0))
    rmat = pl.BlockSpec((D_MODEL, LANES), const)
    return pl.pallas_call(
        _post_body,
        grid=(bsz, nb),
        in_specs=[blk] * 9 + [vec, vec, mat, vec, mat, memb, memb, mat, vec, rmat, rmat,
                               pl.BlockSpec((1, LANES), const)],
        out_specs=[blk, blk, pl.BlockSpec((None, tm, LANES), lambda b, i: (b, i, 0))],
        out_shape=[jax.ShapeDtypeStruct((bsz, l, D_MODEL), F32), jax.ShapeDtypeStruct((bsz, l, D_MODEL), F32),
                   jax.ShapeDtypeStruct((bsz, l, LANES), F32)],
        compiler_params=_cparams(("parallel", "parallel")),
        name="post",
    )(x, oaf, oab, obf, obb, ar, bg, ma, mb, p["gla_onorm"], p["gdn_onorm"], p["w_out"], p["ln_xa"], p["xa_wq"],
      kmem, vmem, p["xa_wo"], p["ln_moe"], p["w_r_hi"], p["w_r_lo"], p["b_r"])


def _route_body(rt_ref, low_ref, rank_ref, cnt_ref, carry):
    @pl.when(pl.program_id(0) == 0)
    def _():
        carry[...] = jnp.zeros_like(carry)

    rt = rt_ref[...]
    lane = lax.broadcasted_iota(I32, rt.shape, 1).astype(F32)
    oh1 = jnp.where(lane == rt[:, _R_E1:_R_E1 + 1], 1.0, 0.0)
    oh2 = jnp.where(lane == rt[:, _R_E2:_R_E2 + 1], 1.0, 0.0)
    tot = oh1 + oh2
    before = _dot(low_ref[...], tot.astype(BF16)) + carry[0:1, :]
    r1 = jnp.sum(before * oh1, axis=-1, keepdims=True)
    r2 = jnp.sum(before * oh2, axis=-1, keepdims=True)
    rank_ref[...] = jnp.where(lane == 0, r1, jnp.where(lane == 1, r2, 0.0))
    new = carry[0:1, :] + jnp.sum(tot, axis=0, keepdims=True)
    carry[...] = jnp.broadcast_to(new, carry.shape)
    cnt_ref[...] = jnp.broadcast_to(new, cnt_ref.shape)


def _route(route2d, tr):
    n = route2d.shape[0]
    i = np.arange(tr)
    low = jnp.asarray(np.where(i[:, None] > i[None, :], 1.0, 0.0), BF16)
    return pl.pallas_call(
        _route_body,
        grid=(n // tr,),
        in_specs=[pl.BlockSpec((tr, LANES), lambda i: (i, 0)), pl.BlockSpec((tr, tr), lambda i: (0, 0))],
        out_specs=[pl.BlockSpec((tr, LANES), lambda i: (i, 0)), pl.BlockSpec((8, LANES), lambda i: (0, 0))],
        out_shape=[jax.ShapeDtypeStruct((n, LANES), F32), jax.ShapeDtypeStruct((8, LANES), F32)],
        scratch_shapes=[pltpu.VMEM((8, LANES), F32)],
        compiler_params=_cparams(("arbitrary",)),
        name="route",
    )(route2d, low)


_ROW_UNROLL = 8


def _scatter_body(dest_ref, hn_ref, xs_in, xs_out, sem):
    del xs_in
    ts = hn_ref.shape[0]

    def row_copy(t, d):
        return pltpu.make_async_copy(hn_ref.at[pl.ds(t, 1)], xs_out.at[pl.ds(d, 1)], sem)

    def issue(t, carry):
        row_copy(t, dest_ref[0, t]).start()
        row_copy(t, dest_ref[1, t]).start()
        return carry

    lax.fori_loop(0, ts, issue, 0, unroll=_ROW_UNROLL)

    def drain(t, carry):
        row_copy(0, 0).wait()
        row_copy(0, 0).wait()
        return carry

    lax.fori_loop(0, ts, drain, 0, unroll=_ROW_UNROLL)


def _scatter(dest, hn2d, xs_init, ts):
    n = hn2d.shape[0]
    return pl.pallas_call(
        _scatter_body,
        grid=(n // ts,),
        in_specs=[pl.BlockSpec((2, ts), lambda i: (0, i), memory_space=pltpu.SMEM),
                  pl.BlockSpec((ts, D_MODEL), lambda i: (i, 0)),
                  pl.BlockSpec(memory_space=pl.ANY)],
        out_specs=pl.BlockSpec(memory_space=pl.ANY),
        out_shape=jax.ShapeDtypeStruct(xs_init.shape, xs_init.dtype),
        scratch_shapes=[pltpu.SemaphoreType.DMA(())],
        input_output_aliases={2: 0},
        compiler_params=_cparams(("arbitrary",)),
        name="scatter",
    )(dest, hn2d, xs_init)


_FFN_BLOCK = 256


def _ffn_body(be_ref, nu_ref, xs_ref, wg_ref, wu_ref, wd_ref, ys_ref, wg_s, wu_s, wd_s):
    i = pl.program_id(0)
    fresh = jnp.logical_or(i == 0, be_ref[i] != be_ref[jnp.maximum(i - 1, 0)])

    @pl.when(fresh)
    def _():
        wg_s[...] = wg_ref[...].astype(BF16)
        wu_s[...] = wu_ref[...].astype(BF16)
        wd_s[...] = wd_ref[...].astype(BF16)

    @pl.when(i < nu_ref[0])
    def _():
        x = xs_ref[...].astype(BF16)
        hidden = _silu(_dot(x, wg_s[...])) * _dot(x, wu_s[...])
        ys_ref[...] = _dot(hidden.astype(BF16), wd_s[...])

    @pl.when(i >= nu_ref[0])
    def _():
        ys_ref[...] = jnp.zeros_like(ys_ref)


def _ffn(block_e, n_used, xs, w_gate, w_up, w_down):
    p_len = xs.shape[0]
    nblk = p_len // _FFN_BLOCK
    grid_spec = pltpu.PrefetchScalarGridSpec(
        num_scalar_prefetch=2,
        grid=(nblk,),
        in_specs=[pl.BlockSpec((_FFN_BLOCK, D_MODEL), lambda i, be, nu: (i, 0)),
                  pl.BlockSpec((None, D_MODEL, D_EXPERT), lambda i, be, nu: (be[i], 0, 0)),
                  pl.BlockSpec((None, D_MODEL, D_EXPERT), lambda i, be, nu: (be[i], 0, 0)),
                  pl.BlockSpec((None, D_EXPERT, D_MODEL), lambda i, be, nu: (be[i], 0, 0))],
        out_specs=pl.BlockSpec((_FFN_BLOCK, D_MODEL), lambda i, be, nu: (i, 0)),
        scratch_shapes=[pltpu.VMEM((D_MODEL, D_EXPERT), BF16), pltpu.VMEM((D_MODEL, D_EXPERT), BF16),
                        pltpu.VMEM((D_EXPERT, D_MODEL), BF16)])
    return pl.pallas_call(
        _ffn_body,
        grid_spec=grid_spec,
        out_shape=jax.ShapeDtypeStruct((p_len, D_MODEL), F32),
        compiler_params=_cparams(("arbitrary",)),
        name="ffn",
    )(block_e, n_used, xs, w_gate, w_up, w_down)


def _combine_body(dest_ref, h2_ref, rt_ref, ys_ref, g_ref, out_ref, buf, sem):
    ts = h2_ref.shape[0]

    def row_copy(k, t, d):
        return pltpu.make_async_copy(ys_ref.at[pl.ds(d, 1)], buf.at[k, pl.ds(t, 1)], sem)

    def issue(t, carry):
        row_copy(0, t, dest_ref[0, t]).start()
        row_copy(1, t, dest_ref[1, t]).start()
        return carry

    lax.fori_loop(0, ts, issue, 0, unroll=_ROW_UNROLL)

    def drain(t, carry):
        row_copy(0, 0, 0).wait()
        row_copy(1, 0, 0).wait()
        return carry

    lax.fori_loop(0, ts, drain, 0, unroll=_ROW_UNROLL)
    rt = rt_ref[...]
    w1 = rt[:, _R_W1:_R_W1 + 1]
    w2 = rt[:, _R_W2:_R_W2 + 1]
    h3 = h2_ref[...] + w1 * buf[0] + w2 * buf[1]
    out_ref[...] = _rmsnorm(h3, g_ref[...])


def _combine(dest, h2, route2d, ys, ln_final, ts):
    n = h2.shape[0]
    return pl.pallas_call(
        _combine_body,
        grid=(n // ts,),
        in_specs=[pl.BlockSpec((2, ts), lambda i: (0, i), memory_space=pltpu.SMEM),
                  pl.BlockSpec((ts, D_MODEL), lambda i: (i, 0)),
                  pl.BlockSpec((ts, LANES), lambda i: (i, 0)),
                  pl.BlockSpec(memory_space=pl.ANY),
                  pl.BlockSpec((1, D_MODEL), lambda i: (0, 0))],
        out_specs=pl.BlockSpec((ts, D_MODEL), lambda i: (i, 0)),
        out_shape=jax.ShapeDtypeStruct((n, D_MODEL), F32),
        scratch_shapes=[pltpu.VMEM((2, ts, D_MODEL), F32), pltpu.SemaphoreType.DMA(())],
        compiler_params=_cparams(("arbitrary",)),
        name="combine",
    )(dest, h2, route2d, ys, ln_final.reshape(1, D_MODEL))


def _moe(h2, hn2, route2d, p):
    n = h2.shape[0]
    rank, counts = _route(route2d, tr=min(512, n))
    counts = counts[0, :N_EXPERTS].astype(I32)
    blk = _FFN_BLOCK
    padded = (counts + blk - 1) // blk * blk
    pends = jnp.cumsum(padded)
    pstart = pends - padded
    nblk = 2 * n // blk + N_EXPERTS
    block_e = jnp.minimum(jnp.searchsorted(pends, jnp.arange(nblk, dtype=I32) * blk, side="right"),
                          N_EXPERTS - 1).astype(I32)
    n_used = (pends[-1:] // blk).astype(I32)
    eid = route2d[:, _R_E1:_R_E2 + 1].astype(I32)
    dest = (pstart[eid] + rank[:, 0:2].astype(I32)).T
    xs = _scatter(dest, hn2, jnp.zeros((nblk * blk, D_MODEL), F32), ts=256)
    ys = _ffn(block_e, n_used, xs, p["moe_w_gate"], p["moe_w_up"], p["moe_w_down"])
    return _combine(dest, h2, route2d, ys, p["ln_final"], ts=256)


def _encode_group(x, mem, p):
    bsz, l, _ = x.shape
    n = bsz * l
    r3 = lambda t: t.reshape(bsz, l, t.shape[-1])
    aq, ak, av, ar, bqkv, bg, ma, mb, lr, gates = _in_proj(
        x.reshape(n, D_MODEL), p["ln_mix"], p["w_main"], p["w_lr"], p["w_g"], tm=min(512, n))
    cq, ck, cv = _gdn_prep(r3(bqkv), p["gdn_conv"], tb=min(512, l))
    oaf, oab = _gla(r3(aq), r3(ak), r3(av), r3(lr), p["gla_w_up"], p["gla_b_up"], tb=min(256, l))
    obf, obb = _gdn(cq, ck, cv, r3(gates), p["gdn_params"], tb=min(256, l))
    kmem, vmem = _mem_kv(mem, p["ln_mem"], p["xa_wk"], p["xa_wv"])
    h2, hn2, route = _post(x, oaf, oab, obf, obb, r3(ar), r3(bg), r3(ma), r3(mb), kmem, vmem, p, tm=min(256, l))
    out = _moe(h2.reshape(n, D_MODEL), hn2.reshape(n, D_MODEL), route.reshape(n, LANES), p)
    return out.reshape(bsz, l, D_MODEL)


def kernel(x_prompt, x_sample, mem_prompt, mem_sample, ln_mix, w_in, gla_w_up_f, gla_b_up_f, gla_w_up_b, gla_b_up_b, gla_onorm, gdn_conv, gdn_a_log_f, gdn_dt_bias_f, gdn_a_log_b, gdn_dt_bias_b, gdn_onorm, w_out, ln_xa, ln_mem, xa_wq, xa_wk, xa_wv, xa_wo, ln_moe, moe_w_group, moe_b_group, moe_w_expert, moe_b_expert, moe_w_gate, moe_w_up, moe_w_down, ln_final):
    w_main, w_lr, w_g = _pack_in_weights(w_in[0])
    gla_w_up, gla_b_up = _pack_gla_up(gla_w_up_f[0], gla_b_up_f[0], gla_w_up_b[0], gla_b_up_b[0])
    vec = lambda t: t.reshape(1, D_MODEL).astype(F32)
    w_r = jnp.concatenate([moe_w_group[0], moe_w_expert[0],
                           jnp.zeros((D_MODEL, LANES - N_GROUPS - N_EXPERTS), F32)], axis=1)
    w_r_hi = w_r.astype(BF16)
    b_r = jnp.concatenate([moe_b_group[0], moe_b_expert[0], jnp.zeros((LANES - N_GROUPS - N_EXPERTS,), F32)])
    p = dict(ln_mix=ln_mix[0], w_main=w_main, w_lr=w_lr, w_g=w_g, gla_w_up=gla_w_up, gla_b_up=gla_b_up,
             gdn_conv=gdn_conv[0],
             gdn_params=_pack_gdn_params(gdn_a_log_f[0], gdn_dt_bias_f[0], gdn_a_log_b[0], gdn_dt_bias_b[0]),
             gla_onorm=vec(jnp.tile(gla_onorm[0], N_HEADS)), gdn_onorm=vec(jnp.tile(gdn_onorm[0], N_HEADS)),
             w_out=w_out[0].astype(BF16), ln_xa=vec(ln_xa[0]), ln_mem=ln_mem[0],
             xa_wq=xa_wq[0].astype(BF16), xa_wk=xa_wk[0].astype(BF16), xa_wv=xa_wv[0].astype(BF16),
             xa_wo=xa_wo[0].astype(BF16), ln_moe=vec(ln_moe[0]),
             w_r_hi=w_r_hi, w_r_lo=(w_r - w_r_hi.astype(F32)).astype(BF16), b_r=b_r.reshape(1, LANES),
             moe_w_gate=moe_w_gate[0], moe_w_up=moe_w_up[0], moe_w_down=moe_w_down[0], ln_final=ln_final)
    return (_encode_group(x_prompt, mem_prompt, p), _encode_group(x_sample, mem_sample, p))
```

```python
import functools
import math

import jax
import jax.numpy as jnp
import numpy as np
from jax import lax
from jax.experimental import pallas as pl
from jax.experimental.pallas import tpu as pltpu

F32 = jnp.float32
BF16 = jnp.bfloat16
I32 = jnp.int32

D_MODEL = 1024
EPS = 1e-6
N_HEADS = 4
DK = 128
DV = 256
QK = N_HEADS * DK
GLA_RANK = 16
GLA_NORMALIZER = 16.0
CHUNK = 64
XA_DH = D_MODEL // N_HEADS
N_GROUPS = 4
EXP_PER_GROUP = 8
N_EXPERTS = N_GROUPS * EXP_PER_GROUP
D_EXPERT = D_MODEL // 2
LANES = 128

VMEM_LIMIT = 56 * 1024 * 1024


def _cparams(sem):
    return pltpu.CompilerParams(dimension_semantics=sem, vmem_limit_bytes=VMEM_LIMIT)


def _dot(a, b):
    return jnp.dot(a, b, preferred_element_type=F32)


def _dot_nt(a, b):
    return lax.dot_general(a, b, (((1,), (1,)), ((), ())), preferred_element_type=F32)


def _dot_tn(a, b):
    return lax.dot_general(a, b, (((0,), (0,)), ((), ())), preferred_element_type=F32)


def _split3(x):
    hi = x.astype(BF16)
    r = x - hi.astype(F32)
    mid = r.astype(BF16)
    lo = (r - mid.astype(F32)).astype(BF16)
    return hi, mid, lo


def _exact_lhs_dot(m_bf16, x):
    hi, mid, lo = _split3(x)
    return _dot(m_bf16, hi) + _dot(m_bf16, mid) + _dot(m_bf16, lo)


def _dot3(a, b):
    a_hi = a.astype(BF16)
    a_lo = (a - a_hi.astype(F32)).astype(BF16)
    b_hi = b.astype(BF16)
    b_lo = (b - b_hi.astype(F32)).astype(BF16)
    return _dot(a_hi, b_hi) + _dot(a_hi, b_lo) + _dot(a_lo, b_hi)


def _sigmoid(x):
    return 1.0 / (1.0 + jnp.exp(-x))


def _silu(x):
    return x * _sigmoid(x)


def _softplus(x):
    return jnp.maximum(x, 0.0) + jnp.log(1.0 + jnp.exp(-jnp.abs(x)))


def _log_sigmoid(x):
    return jnp.minimum(x, 0.0) - jnp.log(1.0 + jnp.exp(-jnp.abs(x)))


_MAIN_WIDTHS = (QK, QK, D_MODEL, D_MODEL, 2 * QK + D_MODEL, D_MODEL, D_MODEL, D_MODEL)
_MAIN_TOTAL = sum(_MAIN_WIDTHS)
_COL_TILE = 512


def _inproj_body(x_ref, g_ref, w_ref, wl_ref, wg_ref, aq, ak, av, ar, bqkv, bg, ma, mb, lr, gates):
    x = x_ref[...]
    ms = jnp.mean(x * x, axis=-1, keepdims=True)
    xn = (x * lax.rsqrt(ms + EPS) * g_ref[...]).astype(BF16)
    col = 0
    for ref in (aq, ak, av, ar, bqkv, bg, ma, mb):
        width = ref.shape[-1]
        for c in range(0, width, _COL_TILE):
            ref[:, c:c + _COL_TILE] = _dot(xn, w_ref[:, col + c:col + c + _COL_TILE]).astype(ref.dtype)
        col += width
    lr[...] = _dot(xn, wl_ref[...])
    gates[...] = _dot(xn, wg_ref[...])


def _pack_in_weights(w_in):
    o = np.cumsum((0, QK, QK, D_MODEL, D_MODEL, GLA_RANK, GLA_RANK,
                   QK, QK, D_MODEL, D_MODEL, N_HEADS, N_HEADS, N_HEADS, N_HEADS, D_MODEL, D_MODEL))
    seg = lambda i: w_in[:, o[i]:o[i + 1]]
    w_main = jnp.concatenate([seg(0), seg(1), seg(2), seg(3), seg(6), seg(7), seg(8), seg(9),
                              seg(14), seg(15)], axis=1).astype(BF16)
    w_lr = jnp.concatenate([seg(4), seg(5), jnp.zeros((D_MODEL, LANES - 2 * GLA_RANK), F32)], axis=1).astype(BF16)
    g4 = jnp.stack([seg(10), seg(11), seg(12), seg(13)], axis=-1)
    w_g = jnp.concatenate([g4, jnp.zeros((D_MODEL, N_HEADS, LANES - 4), F32)], axis=-1)
    w_g = w_g.reshape(D_MODEL, N_HEADS * LANES).astype(BF16)
    return w_main, w_lr, w_g


def _in_proj(x2d, ln_mix, w_main, w_lr, w_g, tm):
    n = x2d.shape[0]
    widths = _MAIN_WIDTHS
    const = lambda i: (0, 0)
    row = lambda i: (i, 0)
    out_shape = [jax.ShapeDtypeStruct((n, w), BF16) for w in widths]
    out_shape += [jax.ShapeDtypeStruct((n, LANES), F32), jax.ShapeDtypeStruct((n, N_HEADS * LANES), F32)]
    out_specs = [pl.BlockSpec((tm, w), row) for w in widths]
    out_specs += [pl.BlockSpec((tm, LANES), row), pl.BlockSpec((tm, N_HEADS * LANES), row)]
    single = pl.Buffered(1)
    return pl.pallas_call(
        _inproj_body,
        grid=(n // tm,),
        in_specs=[pl.BlockSpec((tm, D_MODEL), row),
                  pl.BlockSpec((1, D_MODEL), const),
                  pl.BlockSpec((D_MODEL, _MAIN_TOTAL), const, pipeline_mode=single),
                  pl.BlockSpec((D_MODEL, LANES), const, pipeline_mode=single),
                  pl.BlockSpec((D_MODEL, N_HEADS * LANES), const, pipeline_mode=single)],
        out_specs=out_specs,
        out_shape=out_shape,
        compiler_params=_cparams(("parallel",)),
        name="in_proj",
    )(x2d, ln_mix.reshape(1, D_MODEL), w_main, w_lr, w_g)


def _tri_masks(c):
    r = lax.broadcasted_iota(I32, (c, c), 0)
    s = lax.broadcasted_iota(I32, (c, c), 1)
    return r, s


def _gla_body(qf, kf, vf, lrf, qb, kb, vb, lrb, wf, bf, wb, bb, trif, trib, of, ob, sf, sb):
    @pl.when(pl.program_id(2) == 0)
    def _():
        sf[...] = jnp.zeros_like(sf)
        sb[...] = jnp.zeros_like(sb)

    tb = qf.shape[0]
    c = CHUNK
    nc = tb // c
    r, s = _tri_masks(c)
    streams = []
    for refs, reverse in (((qf, kf, vf, lrf, wf, bf, trif, of, sf), False),
                          ((qb, kb, vb, lrb, wb, bb, trib, ob, sb), True)):
        q_ref, k_ref, v_ref, lr_ref, w_ref, b_ref, tri_ref, o_ref, s_ref = refs
        logits = _dot(lr_ref[...].astype(BF16), w_ref[...]) + b_ref[...]
        gk = _log_sigmoid(logits) * (1.0 / GLA_NORMALIZER)
        cum = _exact_lhs_dot(tri_ref[...], gk)
        streams.append(dict(q=q_ref, k=k_ref, v=v_ref, o=o_ref, s=s_ref, reverse=reverse, cum=cum,
                            incl=(r <= s) if reverse else (r >= s)))
    items = [(st, (nc - 1 - j) if st["reverse"] else j) for j in range(nc) for st in streams]
    work = []
    for st, ci in items:
        r0 = ci * c
        cum = st["cum"][r0:r0 + c]
        edge = r0 if st["reverse"] else r0 + c - 1
        tot = st["cum"][edge:edge + 1]
        q = st["q"][r0:r0 + c, :].astype(F32)
        k = st["k"][r0:r0 + c, :].astype(F32)
        work.append(dict(r0=r0, v=st["v"][r0:r0 + c, :],
                         q_dec=(q * jnp.exp(cum) * (DK ** -0.5)).astype(BF16),
                         k_inv=(k * jnp.exp(-cum)).astype(BF16),
                         k_end=(k * jnp.exp(tot - cum)).astype(BF16),
                         dec=jnp.transpose(jnp.broadcast_to(jnp.exp(tot), (DK, DK)))))
    att = [jnp.where(st["incl"], _dot_nt(y["q_dec"], y["k_inv"]), 0.0).astype(BF16) for (st, _), y in zip(items, work)]
    intra = [_dot(a, y["v"]) for a, y in zip(att, work)]
    upd = [_dot_tn(y["k_end"], y["v"]) for y in work]
    for (st, ci), y, o_intra, u in zip(items, work, intra, upd):
        r0 = y["r0"]
        s_ref = st["s"]
        state = s_ref[...]
        o = o_intra + _dot(y["q_dec"], state.astype(BF16))
        st["o"][r0:r0 + c, :] = o.astype(st["o"].dtype)
        s_ref[:, 0:DK] = state[:, 0:DK] * y["dec"] + u[:, 0:DK]
        s_ref[:, DK:DV] = state[:, DK:DV] * y["dec"] + u[:, DK:DV]


def _gla(aq, ak, av, lr, w_up, b_up, tb):
    bsz, l, _ = aq.shape
    nb = l // tb
    fwd = lambda b, h, i: (b, i, h)
    bwd = lambda b, h, i: (b, nb - 1 - i, h)
    fwd0 = lambda b, h, i: (b, i, 0)
    bwd0 = lambda b, h, i: (b, nb - 1 - i, 0)
    qk_spec = lambda m: pl.BlockSpec((None, tb, DK), m)
    v_spec = lambda m: pl.BlockSpec((None, tb, DV), m)
    lr_spec = lambda m: pl.BlockSpec((None, tb, LANES), m)
    in_specs = [qk_spec(fwd), qk_spec(fwd), v_spec(fwd), lr_spec(fwd0),
                qk_spec(bwd), qk_spec(bwd), v_spec(bwd), lr_spec(bwd0),
                pl.BlockSpec((LANES, DK), lambda b, h, i: (0, h)),
                pl.BlockSpec((1, DK), lambda b, h, i: (0, h)),
                pl.BlockSpec((LANES, DK), lambda b, h, i: (0, N_HEADS + h)),
                pl.BlockSpec((1, DK), lambda b, h, i: (0, N_HEADS + h)),
                pl.BlockSpec((tb, tb), lambda b, h, i: (0, 0)), pl.BlockSpec((tb, tb), lambda b, h, i: (0, 0))]
    out = jax.ShapeDtypeStruct((bsz, l, D_MODEL), BF16)
    return pl.pallas_call(
        _gla_body,
        grid=(bsz, N_HEADS, nb),
        in_specs=in_specs,
        out_specs=[v_spec(fwd), v_spec(bwd)],
        out_shape=[out, out],
        scratch_shapes=[pltpu.VMEM((DK, DV), F32), pltpu.VMEM((DK, DV), F32)],
        compiler_params=_cparams(("parallel", "parallel", "arbitrary")),
        name="gla",
    )(aq, ak, av, lr, aq, ak, av, lr, w_up, b_up, w_up, b_up, _chunk_tri(tb, False), _chunk_tri(tb, True))


def _pack_gla_up(w_up_f, b_up_f, w_up_b, b_up_b):
    w = jnp.zeros((LANES, 2 * QK), F32)
    w = w.at[0:GLA_RANK, 0:QK].set(w_up_f).at[GLA_RANK:2 * GLA_RANK, QK:2 * QK].set(w_up_b)
    b = jnp.concatenate([b_up_f, b_up_b]).reshape(1, 2 * QK).astype(F32)
    return w.astype(BF16), b


_HALO = 16


def _gdn_prep_body(x_ref, prev_ref, next_ref, w_ref, q_ref, k_ref, v_ref):
    i = pl.program_id(1)
    nb = pl.num_programs(1)
    tb = x_ref.shape[0]
    x = x_ref[...].astype(F32)
    prev_row = jnp.where(i > 0, prev_ref[_HALO - 1:_HALO, :].astype(F32), 0.0)
    next_row = jnp.where(i < nb - 1, next_ref[0:1, :].astype(F32), 0.0)
    ridx = lax.broadcasted_iota(I32, (tb, 1), 0)
    x_prev = jnp.where(ridx == 0, prev_row, pltpu.roll(x, 1, axis=0))
    x_next = jnp.where(ridx == tb - 1, next_row, pltpu.roll(x, tb - 1, axis=0))
    y = _silu(x_prev * w_ref[0:1, :] + x * w_ref[1:2, :] + x_next * w_ref[2:3, :])
    for h in range(N_HEADS):
        qh = y[:, h * DK:(h + 1) * DK]
        q_ref[:, h * DK:(h + 1) * DK] = (
            qh * (lax.rsqrt(jnp.sum(qh * qh, axis=-1, keepdims=True) + EPS) * DK ** -0.5)).astype(q_ref.dtype)
        kh = y[:, QK + h * DK:QK + (h + 1) * DK]
        k_ref[:, h * DK:(h + 1) * DK] = (
            kh * lax.rsqrt(jnp.sum(kh * kh, axis=-1, keepdims=True) + EPS)).astype(k_ref.dtype)
    v_ref[...] = y[:, 2 * QK:].astype(v_ref.dtype)


def _gdn_prep(bqkv, conv_w, tb):
    bsz, l, width = bqkv.shape
    nb = l // tb
    hb = tb // _HALO
    nh = l // _HALO
    return pl.pallas_call(
        _gdn_prep_body,
        grid=(bsz, nb),
        in_specs=[pl.BlockSpec((None, tb, width), lambda b, i: (b, i, 0)),
                  pl.BlockSpec((None, _HALO, width), lambda b, i: (b, jnp.maximum(i * hb - 1, 0), 0)),
                  pl.BlockSpec((None, _HALO, width), lambda b, i: (b, jnp.minimum((i + 1) * hb, nh - 1), 0)),
                  pl.BlockSpec((3, width), lambda b, i: (0, 0))],
        out_specs=[pl.BlockSpec((None, tb, QK), lambda b, i: (b, i, 0)),
                   pl.BlockSpec((None, tb, QK), lambda b, i: (b, i, 0)),
                   pl.BlockSpec((None, tb, D_MODEL), lambda b, i: (b, i, 0))],
        out_shape=[jax.ShapeDtypeStruct((bsz, l, QK), BF16), jax.ShapeDtypeStruct((bsz, l, QK), BF16),
                   jax.ShapeDtypeStruct((bsz, l, D_MODEL), BF16)],
        compiler_params=_cparams(("parallel", "parallel")),
        name="gdn_prep",
    )(bqkv, bqkv, bqkv, conv_w.astype(F32))


_G_BETA = 0
_G_A = 2


def _pack_gdn_params(a_log_f, dt_bias_f, a_log_b, dt_bias_b):
    p = jnp.zeros((8, N_HEADS, LANES), F32)
    p = p.at[0, :, _G_A].set(-jnp.exp(a_log_f.astype(F32))).at[0, :, _G_A + 1].set(-jnp.exp(a_log_b.astype(F32)))
    p = p.at[1, :, _G_A].set(dt_bias_f.astype(F32)).at[1, :, _G_A + 1].set(dt_bias_b.astype(F32))
    return p.reshape(8, N_HEADS * LANES)


def _split2(x):
    hi = x.astype(BF16)
    return hi, (x - hi.astype(F32)).astype(BF16)


def _dot3s(a, b):
    a_hi, a_lo = a
    b_hi, b_lo = b
    lhs = jnp.concatenate([a_hi, a_hi, a_lo], axis=1)
    rhs = jnp.concatenate([b_hi, b_lo, b_hi], axis=0)
    return _dot(lhs, rhs)


def _unit_tri_inverse_many(a_list, r, s, eye):
    same16 = (r >> 4) == (s >> 4)
    same32 = (r >> 5) == (s >> 5)
    off16 = jnp.logical_and(same32, jnp.logical_not(same16))
    d = [jnp.where(same16, a, 0.0) for a in a_list]
    n = [eye - x for x in d]
    ds = [_split2(x) for x in d]
    p = [_dot3s(x, x) for x in ds]
    for level in range(3):
        ps = [_split2(x) for x in p]
        n = [x + _dot3s(_split2(x), y) for x, y in zip(n, ps)]
        if level < 2:
            p = [_dot3s(y, y) for y in ps]
    for e in ([jnp.where(off16, a, 0.0) for a in a_list], [jnp.where(same32, 0.0, a) for a in a_list]):
        es = [_split2(x) for x in e]
        ns = [_split2(x) for x in n]
        t = [_dot3s(x, y) for x, y in zip(ns, es)]
        n = [x - _dot3s(_split2(y), z) for x, y, z in zip(n, t, ns)]
    return n


def _gdn_body(qf, kf, vf, gf, qb, kb, vb, gb, p_ref, trif, trib, of, ob, sf, sb):
    @pl.when(pl.program_id(2) == 0)
    def _():
        sf[...] = jnp.zeros_like(sf)
        sb[...] = jnp.zeros_like(sb)

    tb = qf.shape[0]
    c = CHUNK
    nc = tb // c
    r, s = _tri_masks(c)
    eye = jnp.where(r == s, 1.0, 0.0)
    streams = []
    for refs, reverse in (((qf, kf, vf, gf, trif, of, sf), False), ((qb, kb, vb, gb, trib, ob, sb), True)):
        q_ref, k_ref, v_ref, g_ref, tri_ref, o_ref, s_ref = refs
        gates = g_ref[...]
        g_all = p_ref[0:1, :] * _softplus(gates + p_ref[1:2, :])
        cum_all = _exact_lhs_dot(tri_ref[...], g_all)
        streams.append(dict(q=q_ref, k=k_ref, v=v_ref, o=o_ref, s=s_ref, reverse=reverse,
                            beta=_sigmoid(gates), cum=cum_all, cum_t=jnp.transpose(cum_all),
                            incl=(r <= s) if reverse else (r >= s), strict=(r < s) if reverse else (r > s),
                            la=_G_A + int(reverse), lb=_G_BETA + int(reverse)))
    items = [(st, (nc - 1 - j) if st["reverse"] else j) for j in range(nc) for st in streams]

    pre = []
    for st, ci in items:
        r0 = ci * c
        la, lb = st["la"], st["lb"]
        ccol = st["cum"][r0:r0 + c, la:la + 1]
        crow = st["cum_t"][la:la + 1, r0:r0 + c]
        edge = r0 if st["reverse"] else r0 + c - 1
        tot = st["cum"][edge:edge + 1, la:la + 1]
        beta = st["beta"][r0:r0 + c, lb:lb + 1]
        decay = jnp.where(st["incl"], jnp.exp(jnp.where(st["incl"], ccol - crow, 0.0)), 0.0)
        pre.append(dict(ccol=ccol, tot=tot, beta=beta, decay=decay, r0=r0,
                        qb=st["q"][r0:r0 + c, :], kb=st["k"][r0:r0 + c, :]))
    kk = [_dot_nt(x["kb"], x["kb"]) for x in pre]
    a_list = [jnp.where(st["strict"], y * x["beta"] * x["decay"], 0.0) for (st, _), x, y in zip(items, pre, kk)]
    t_inv = _unit_tri_inverse_many(a_list, r, s, eye)
    work = []
    for (st, ci), x, t in zip(items, pre, t_inv):
        r0 = x["r0"]
        k = x["kb"].astype(F32)
        e_cum = jnp.exp(x["ccol"])
        t_hi, t_lo = _split2(t)
        rhs = jnp.concatenate([(st["v"][r0:r0 + c, :].astype(F32) * x["beta"]).astype(BF16),
                               (k * (x["beta"] * e_cum)).astype(BF16)], axis=1)
        uw = _dot(t_hi, rhs) + _dot(t_lo, rhs)
        work.append(dict(u=uw[:, 0:DV], w=uw[:, DV:DV + DK].astype(BF16),
                         att=(_dot_nt(x["qb"], x["kb"]) * x["decay"]).astype(BF16),
                         q_dec=(x["qb"].astype(F32) * e_cum).astype(BF16),
                         k_end=(k * jnp.exp(x["tot"] - x["ccol"])).astype(BF16),
                         dec=jnp.exp(x["tot"])))
    for (st, ci), x, y in zip(items, pre, work):
        r0 = x["r0"]
        s_ref = st["s"]
        state = s_ref[...]
        sbf = state.astype(BF16)
        v_new = (y["u"] - _dot(y["w"], sbf)).astype(BF16)
        o = _dot(y["q_dec"], sbf) + _dot(y["att"], v_new)
        st["o"][r0:r0 + c, :] = o.astype(st["o"].dtype)
        s_ref[...] = state * y["dec"] + _dot_tn(y["k_end"], v_new)


def _chunk_tri(tb, reverse):
    i = np.arange(tb)
    same = (i[:, None] // CHUNK) == (i[None, :] // CHUNK)
    tri = (i[:, None] <= i[None, :]) if reverse else (i[:, None] >= i[None, :])
    return jnp.asarray(np.where(same & tri, 1.0, 0.0), BF16)


def _gdn(cq, ck, cv, gates, gparams, tb):
    bsz, l, _ = cq.shape
    nb = l // tb
    fwd = lambda b, h, i: (b, i, h)
    bwd = lambda b, h, i: (b, nb - 1 - i, h)
    qk_spec = lambda m: pl.BlockSpec((None, tb, DK), m)
    v_spec = lambda m: pl.BlockSpec((None, tb, DV), m)
    g_spec = lambda m: pl.BlockSpec((None, tb, LANES), m)
    const = lambda b, h, i: (0, 0)
    in_specs = [qk_spec(fwd), qk_spec(fwd), v_spec(fwd), g_spec(fwd),
                qk_spec(bwd), qk_spec(bwd), v_spec(bwd), g_spec(bwd),
                pl.BlockSpec((8, LANES), lambda b, h, i: (0, h)),
                pl.BlockSpec((tb, tb), const), pl.BlockSpec((tb, tb), const)]
    out = jax.ShapeDtypeStruct((bsz, l, D_MODEL), BF16)
    return pl.pallas_call(
        _gdn_body,
        grid=(bsz, N_HEADS, nb),
        in_specs=in_specs,
        out_specs=[v_spec(fwd), v_spec(bwd)],
        out_shape=[out, out],
        scratch_shapes=[pltpu.VMEM((DK, DV), F32), pltpu.VMEM((DK, DV), F32)],
        compiler_params=_cparams(("parallel", "parallel", "arbitrary")),
        name="gdn",
    )(cq, ck, cv, gates, cq, ck, cv, gates, gparams, _chunk_tri(tb, False), _chunk_tri(tb, True))


def _rmsnorm(x, w):
    return x * lax.rsqrt(jnp.mean(x * x, axis=-1, keepdims=True) + EPS) * w


def _mem_kv_body(m_ref, g_ref, wk_ref, wv_ref, k_ref, v_ref):
    mn = _rmsnorm(m_ref[...], g_ref[...]).astype(BF16)
    k_ref[...] = _dot(mn, wk_ref[...]).astype(k_ref.dtype)
    v_ref[...] = _dot(mn, wv_ref[...]).astype(v_ref.dtype)


def _mem_kv(mem, ln_mem, wk, wv):
    bsz, m, _ = mem.shape
    const = lambda b: (0, 0)
    blk = pl.BlockSpec((None, m, D_MODEL), lambda b: (b, 0, 0))
    out = jax.ShapeDtypeStruct((bsz, m, D_MODEL), BF16)
    return pl.pallas_call(
        _mem_kv_body,
        grid=(bsz,),
        in_specs=[blk, pl.BlockSpec((1, D_MODEL), const),
                  pl.BlockSpec((D_MODEL, D_MODEL), const), pl.BlockSpec((D_MODEL, D_MODEL), const)],
        out_specs=[blk, blk],
        out_shape=[out, out],
        compiler_params=_cparams(("parallel",)),
        name="mem_kv",
    )(mem, ln_mem.reshape(1, D_MODEL), wk, wv)


_R_E1, _R_E2, _R_W1, _R_W2 = 0, 1, 2, 3


def _head_rmsnorm(o, w):
    parts = []
    for h in range(N_HEADS):
        oh = o[:, h * DV:(h + 1) * DV]
        parts.append(oh * lax.rsqrt(jnp.mean(oh * oh, axis=-1, keepdims=True) + EPS))
    return jnp.concatenate(parts, axis=-1) * w


def _route_tile(lg):
    neg = -1e30
    big = 1e9
    lane = lax.broadcasted_iota(I32, lg.shape, 1).astype(F32)
    gmask = lane < N_GROUPS
    gl = jnp.where(gmask, lg, neg)
    gmax = jnp.max(gl, axis=-1, keepdims=True)
    gidx = jnp.min(jnp.where(gl == gmax, lane, big), axis=-1, keepdims=True)
    gsum = jnp.sum(jnp.where(gmask, jnp.exp(gl - gmax), 0.0), axis=-1, keepdims=True)
    lo = N_GROUPS + EXP_PER_GROUP * gidx
    off = lane - lo
    el = jnp.where(jnp.abs(off - (EXP_PER_GROUP - 1) * 0.5) < EXP_PER_GROUP * 0.5, lg, neg)
    m1 = jnp.max(el, axis=-1, keepdims=True)
    i1 = jnp.min(jnp.where(el == m1, lane, big), axis=-1, keepdims=True)
    el2 = jnp.where(lane == i1, neg, el)
    m2 = jnp.max(el2, axis=-1, keepdims=True)
    i2 = jnp.min(jnp.where(el2 == m2, lane, big), axis=-1, keepdims=True)
    r = jnp.exp(m2 - m1)
    p1 = 1.0 / (1.0 + r)
    gw = 1.0 / gsum
    w1 = gw * p1
    w2 = gw * (r * p1)
    e1 = i1 - N_GROUPS
    e2 = i2 - N_GROUPS
    return jnp.where(lane == _R_E1, e1, jnp.where(lane == _R_E2, e2,
                     jnp.where(lane == _R_W1, w1, jnp.where(lane == _R_W2, w2, 0.0))))


def _post_body(x_ref, oaf, oab, obf, obb, ar, bg, ma, mb, na_ref, nb_ref, wout, lnxa, wq, kmem, vmem, wo, lnmoe,
               wrh, wrl, br, h2_ref, hn2_ref, route_ref):
    f = lambda ref: ref[...].astype(F32)
    oa = _head_rmsnorm(f(oaf) + f(oab), na_ref[...]) * _silu(f(ar))
    ob = _head_rmsnorm(f(obf) + f(obb), nb_ref[...]) * _silu(f(bg))
    y = _sigmoid(f(ma)) * oa + _sigmoid(f(mb)) * ob
    h1 = x_ref[...] + _dot(y.astype(BF16), wout[...])
    hn = _rmsnorm(h1, lnxa[...]).astype(BF16)
    q = _dot(hn, wq[...])
    outs = []
    for h in range(N_HEADS):
        sl = slice(h * XA_DH, (h + 1) * XA_DH)
        s = _dot_nt(q[:, sl].astype(BF16), kmem[:, sl]) * (XA_DH ** -0.5)
        p = jnp.exp(s - jnp.max(s, axis=-1, keepdims=True))
        denom = jnp.sum(p, axis=-1, keepdims=True)
        outs.append(_dot(p.astype(BF16), vmem[:, sl]) * (1.0 / denom))
    o = jnp.concatenate(outs, axis=-1)
    h2 = h1 + _dot(o.astype(BF16), wo[...])
    h2_ref[...] = h2
    hn2 = _rmsnorm(h2, lnmoe[...])
    hn2_ref[...] = hn2
    x_hi = hn2.astype(BF16)
    x_lo = (hn2 - x_hi.astype(F32)).astype(BF16)
    lg = _dot(x_hi, wrh[...]) + _dot(x_hi, wrl[...]) + _dot(x_lo, wrh[...]) + br[...]
    route_ref[...] = _route_tile(lg)


def _post(x, oaf, oab, obf, obb, ar, bg, ma, mb, kmem, vmem, p, tm):
    bsz, l, _ = x.shape
    nb = l // tm
    blk = pl.BlockSpec((None, tm, D_MODEL), lambda b, i: (b, i, 0))
    const = lambda b, i: (0, 0)
    vec = pl.BlockSpec((1, D_MODEL), const)
    mat = pl.BlockSpec((D_MODEL, D_MODEL), const, pipeline_mode=pl.Buffered(1))
    memb = pl.BlockSpec((None, kmem.shape[1], D_MODEL), lambda b, i: (b, 0, 0))
    rmat = pl.BlockSpec((D_MODEL, LANES), const)
    return pl.pallas_call(
        _post_body,
        grid=(bsz, nb),
        in_specs=[blk] * 9 + [vec, vec, mat, vec, mat, memb, memb, mat, vec, rmat, rmat,
                               pl.BlockSpec((1, LANES), const)],
        out_specs=[blk, blk, pl.BlockSpec((None, tm, LANES), lambda b, i: (b, i, 0))],
        out_shape=[jax.ShapeDtypeStruct((bsz, l, D_MODEL), F32), jax.ShapeDtypeStruct((bsz, l, D_MODEL), F32),
                   jax.ShapeDtypeStruct((bsz, l, LANES), F32)],
        compiler_params=_cparams(("parallel", "parallel")),
        name="post",
    )(x, oaf, oab, obf, obb, ar, bg, ma, mb, p["gla_onorm"], p["gdn_onorm"], p["w_out"], p["ln_xa"], p["xa_wq"],
      kmem, vmem, p["xa_wo"], p["ln_moe"], p["w_r_hi"], p["w_r_lo"], p["b_r"])


def _route_body(rt_ref, low_ref, rank_ref, cnt_ref, carry):
    @pl.when(pl.program_id(0) == 0)
    def _():
        carry[...] = jnp.zeros_like(carry)

    rt = rt_ref[...]
    lane = lax.broadcasted_iota(I32, rt.shape, 1).astype(F32)
    oh1 = jnp.where(lane == rt[:, _R_E1:_R_E1 + 1], 1.0, 0.0)
    oh2 = jnp.where(lane == rt[:, _R_E2:_R_E2 + 1], 1.0, 0.0)
    tot = oh1 + oh2
    before = _dot(low_ref[...], tot.astype(BF16)) + carry[0:1, :]
    r1 = jnp.sum(before * oh1, axis=-1, keepdims=True)
    r2 = jnp.sum(before * oh2, axis=-1, keepdims=True)
    rank_ref[...] = jnp.where(lane == 0, r1, jnp.where(lane == 1, r2, 0.0))
    new = carry[0:1, :] + jnp.sum(tot, axis=0, keepdims=True)
    carry[...] = jnp.broadcast_to(new, carry.shape)
    cnt_ref[...] = jnp.broadcast_to(new, cnt_ref.shape)


def _route(route2d, tr):
    n = route2d.shape[0]
    i = np.arange(tr)
    low = jnp.asarray(np.where(i[:, None] > i[None, :], 1.0, 0.0), BF16)
    return pl.pallas_call(
        _route_body,
        grid=(n // tr,),
        in_specs=[pl.BlockSpec((tr, LANES), lambda i: (i, 0)), pl.BlockSpec((tr, tr), lambda i: (0, 0))],
        out_specs=[pl.BlockSpec((tr, LANES), lambda i: (i, 0)), pl.BlockSpec((8, LANES), lambda i: (0, 0))],
        out_shape=[jax.ShapeDtypeStruct((n, LANES), F32), jax.ShapeDtypeStruct((8, LANES), F32)],
        scratch_shapes=[pltpu.VMEM((8, LANES), F32)],
        compiler_params=_cparams(("arbitrary",)),
        name="route",
    )(route2d, low)


_ROW_UNROLL = 8


def _scatter_body(dest_ref, hn_ref, xs_in, xs_out, sem):
    del xs_in
    ts = hn_ref.shape[0]

    def row_copy(t, d):
        return pltpu.make_async_copy(hn_ref.at[pl.ds(t, 1)], xs_out.at[pl.ds(d, 1)], sem)

    def issue(t, carry):
        row_copy(t, dest_ref[0, t]).start()
        row_copy(t, dest_ref[1, t]).start()
        return carry

    lax.fori_loop(0, ts, issue, 0, unroll=_ROW_UNROLL)

    def drain(t, carry):
        row_copy(0, 0).wait()
        row_copy(0, 0).wait()
        return carry

    lax.fori_loop(0, ts, drain, 0, unroll=_ROW_UNROLL)


def _scatter(dest, hn2d, xs_init, ts):
    n = hn2d.shape[0]
    return pl.pallas_call(
        _scatter_body,
        grid=(n // ts,),
        in_specs=[pl.BlockSpec((2, ts), lambda i: (0, i), memory_space=pltpu.SMEM),
                  pl.BlockSpec((ts, D_MODEL), lambda i: (i, 0)),
                  pl.BlockSpec(memory_space=pl.ANY)],
        out_specs=pl.BlockSpec(memory_space=pl.ANY),
        out_shape=jax.ShapeDtypeStruct(xs_init.shape, xs_init.dtype),
        scratch_shapes=[pltpu.SemaphoreType.DMA(())],
        input_output_aliases={2: 0},
        compiler_params=_cparams(("arbitrary",)),
        name="scatter",
    )(dest, hn2d, xs_init)


_FFN_BLOCK = 256


def _ffn_body(be_ref, nu_ref, xs_ref, wg_ref, wu_ref, wd_ref, ys_ref, wg_s, wu_s, wd_s):
    i = pl.program_id(0)
    fresh = jnp.logical_or(i == 0, be_ref[i] != be_ref[jnp.maximum(i - 1, 0)])

    @pl.when(fresh)
    def _():
        wg_s[...] = wg_ref[...].astype(BF16)
        wu_s[...] = wu_ref[...].astype(BF16)
        wd_s[...] = wd_ref[...].astype(BF16)

    @pl.when(i < nu_ref[0])
    def _():
        x = xs_ref[...].astype(BF16)
        hidden = _silu(_dot(x, wg_s[...])) * _dot(x, wu_s[...])
        ys_ref[...] = _dot(hidden.astype(BF16), wd_s[...])

    @pl.when(i >= nu_ref[0])
    def _():
        ys_ref[...] = jnp.zeros_like(ys_ref)


def _ffn(block_e, n_used, xs, w_gate, w_up, w_down):
    p_len = xs.shape[0]
    nblk = p_len // _FFN_BLOCK
    grid_spec = pltpu.PrefetchScalarGridSpec(
        num_scalar_prefetch=2,
        grid=(nblk,),
        in_specs=[pl.BlockSpec((_FFN_BLOCK, D_MODEL), lambda i, be, nu: (i, 0)),
                  pl.BlockSpec((None, D_MODEL, D_EXPERT), lambda i, be, nu: (be[i], 0, 0)),
                  pl.BlockSpec((None, D_MODEL, D_EXPERT), lambda i, be, nu: (be[i], 0, 0)),
                  pl.BlockSpec((None, D_EXPERT, D_MODEL), lambda i, be, nu: (be[i], 0, 0))],
        out_specs=pl.BlockSpec((_FFN_BLOCK, D_MODEL), lambda i, be, nu: (i, 0)),
        scratch_shapes=[pltpu.VMEM((D_MODEL, D_EXPERT), BF16), pltpu.VMEM((D_MODEL, D_EXPERT), BF16),
                        pltpu.VMEM((D_EXPERT, D_MODEL), BF16)])
    return pl.pallas_call(
        _ffn_body,
        grid_spec=grid_spec,
        out_shape=jax.ShapeDtypeStruct((p_len, D_MODEL), F32),
        compiler_params=_cparams(("arbitrary",)),
        name="ffn",
    )(block_e, n_used, xs, w_gate, w_up, w_down)


def _combine_body(dest_ref, h2_ref, rt_ref, ys_ref, g_ref, out_ref, buf, sem):
    ts = h2_ref.shape[0]

    def row_copy(k, t, d):
        return pltpu.make_async_copy(ys_ref.at[pl.ds(d, 1)], buf.at[k, pl.ds(t, 1)], sem)

    def issue(t, carry):
        row_copy(0, t, dest_ref[0, t]).start()
        row_copy(1, t, dest_ref[1, t]).start()
        return carry

    lax.fori_loop(0, ts, issue, 0, unroll=_ROW_UNROLL)

    def drain(t, carry):
        row_copy(0, 0, 0).wait()
        row_copy(1, 0, 0).wait()
        return carry

    lax.fori_loop(0, ts, drain, 0, unroll=_ROW_UNROLL)
    rt = rt_ref[...]
    w1 = rt[:, _R_W1:_R_W1 + 1]
    w2 = rt[:, _R_W2:_R_W2 + 1]
    h3 = h2_ref[...] + w1 * buf[0] + w2 * buf[1]
    out_ref[...] = _rmsnorm(h3, g_ref[...])


def _combine(dest, h2, route2d, ys, ln_final, ts):
    n = h2.shape[0]
    return pl.pallas_call(
        _combine_body,
        grid=(n // ts,),
        in_specs=[pl.BlockSpec((2, ts), lambda i: (0, i), memory_space=pltpu.SMEM),
                  pl.BlockSpec((ts, D_MODEL), lambda i: (i, 0)),
                  pl.BlockSpec((ts, LANES), lambda i: (i, 0)),
                  pl.BlockSpec(memory_space=pl.ANY),
                  pl.BlockSpec((1, D_MODEL), lambda i: (0, 0))],
        out_specs=pl.BlockSpec((ts, D_MODEL), lambda i: (i, 0)),
        out_shape=jax.ShapeDtypeStruct((n, D_MODEL), F32),
        scratch_shapes=[pltpu.VMEM((2, ts, D_MODEL), F32), pltpu.SemaphoreType.DMA(())],
        compiler_params=_cparams(("arbitrary",)),
        name="combine",
    )(dest, h2, route2d, ys, ln_final.reshape(1, D_MODEL))


def _moe(h2, hn2, route2d, p):
    n = h2.shape[0]
    rank, counts = _route(route2d, tr=min(512, n))
    counts = counts[0, :N_EXPERTS].astype(I32)
    blk = _FFN_BLOCK
    padded = (counts + blk - 1) // blk * blk
    pends = jnp.cumsum(padded)
    pstart = pends - padded
    nblk = 2 * n // blk + N_EXPERTS
    starts = jnp.arange(nblk, dtype=I32) * blk
    block_e = jnp.minimum(jnp.sum((pends[None, :] <= starts[:, None]).astype(I32), axis=1), N_EXPERTS - 1)
    n_used = (pends[-1:] // blk).astype(I32)
    eid = route2d[:, _R_E1:_R_E2 + 1].astype(I32)
    dest = (pstart[eid] + rank[:, 0:2].astype(I32)).T
    xs = _scatter(dest, hn2, jnp.zeros((nblk * blk, D_MODEL), F32), ts=256)
    ys = _ffn(block_e, n_used, xs, p["moe_w_gate"], p["moe_w_up"], p["moe_w_down"])
    return _combine(dest, h2, route2d, ys, p["ln_final"], ts=256)


def _encode_group(x, mem, p):
    bsz, l, _ = x.shape
    n = bsz * l
    r3 = lambda t: t.reshape(bsz, l, t.shape[-1])
    aq, ak, av, ar, bqkv, bg, ma, mb, lr, gates = _in_proj(
        x.reshape(n, D_MODEL), p["ln_mix"], p["w_main"], p["w_lr"], p["w_g"], tm=min(512, n))
    cq, ck, cv = _gdn_prep(r3(bqkv), p["gdn_conv"], tb=min(512, l))
    oaf, oab = _gla(r3(aq), r3(ak), r3(av), r3(lr), p["gla_w_up"], p["gla_b_up"], tb=min(512, l))
    obf, obb = _gdn(cq, ck, cv, r3(gates), p["gdn_params"], tb=min(512, l))
    kmem, vmem = _mem_kv(mem, p["ln_mem"], p["xa_wk"], p["xa_wv"])
    h2, hn2, route = _post(x, oaf, oab, obf, obb, r3(ar), r3(bg), r3(ma), r3(mb), kmem, vmem, p, tm=min(512, l))
    out = _moe(h2.reshape(n, D_MODEL), hn2.reshape(n, D_MODEL), route.reshape(n, LANES), p)
    return out.reshape(bsz, l, D_MODEL)


def kernel(x_prompt, x_sample, mem_prompt, mem_sample, ln_mix, w_in, gla_w_up_f, gla_b_up_f, gla_w_up_b, gla_b_up_b, gla_onorm, gdn_conv, gdn_a_log_f, gdn_dt_bias_f, gdn_a_log_b, gdn_dt_bias_b, gdn_onorm, w_out, ln_xa, ln_mem, xa_wq, xa_wk, xa_wv, xa_wo, ln_moe, moe_w_group, moe_b_group, moe_w_expert, moe_b_expert, moe_w_gate, moe_w_up, moe_w_down, ln_final):
    w_main, w_lr, w_g = _pack_in_weights(w_in[0])
    gla_w_up, gla_b_up = _pack_gla_up(gla_w_up_f[0], gla_b_up_f[0], gla_w_up_b[0], gla_b_up_b[0])
    vec = lambda t: t.reshape(1, D_MODEL).astype(F32)
    w_r = jnp.concatenate([moe_w_group[0], moe_w_expert[0],
                           jnp.zeros((D_MODEL, LANES - N_GROUPS - N_EXPERTS), F32)], axis=1)
    w_r_hi = w_r.astype(BF16)
    b_r = jnp.concatenate([moe_b_group[0], moe_b_expert[0], jnp.zeros((LANES - N_GROUPS - N_EXPERTS,), F32)])
    p = dict(ln_mix=ln_mix[0], w_main=w_main, w_lr=w_lr, w_g=w_g, gla_w_up=gla_w_up, gla_b_up=gla_b_up,
             gdn_conv=gdn_conv[0],
             gdn_params=_pack_gdn_params(gdn_a_log_f[0], gdn_dt_bias_f[0], gdn_a_log_b[0], gdn_dt_bias_b[0]),
             gla_onorm=vec(jnp.tile(gla_onorm[0], N_HEADS)), gdn_onorm=vec(jnp.tile(gdn_onorm[0], N_HEADS)),
             w_out=w_out[0].astype(BF16), ln_xa=vec(ln_xa[0]), ln_mem=ln_mem[0],
             xa_wq=xa_wq[0].astype(BF16), xa_wk=xa_wk[0].astype(BF16), xa_wv=xa_wv[0].astype(BF16),
             xa_wo=xa_wo[0].astype(BF16), ln_moe=vec(ln_moe[0]),
             w_r_hi=w_r_hi, w_r_lo=(w_r - w_r_hi.astype(F32)).astype(BF16), b_r=b_r.reshape(1, LANES),
             moe_w_gate=moe_w_gate[0], moe_w_up=moe_w_up[0], moe_w_down=moe_w_down[0], ln_final=ln_final)
    return (_encode_group(x_prompt, mem_prompt, p), _encode_group(x_sample, mem_sample, p))
```

```python
import functools
import math

import jax
import jax.numpy as jnp
import numpy as np
from jax import lax
from jax.experimental import pallas as pl
from jax.experimental.pallas import tpu as pltpu

F32 = jnp.float32
BF16 = jnp.bfloat16
I32 = jnp.int32

D_MODEL = 1024
EPS = 1e-6
N_HEADS = 4
DK = 128
DV = 256
QK = N_HEADS * DK
GLA_RANK = 16
GLA_NORMALIZER = 16.0
CHUNK = 64
XA_DH = D_MODEL // N_HEADS
N_GROUPS = 4
EXP_PER_GROUP = 8
N_EXPERTS = N_GROUPS * EXP_PER_GROUP
D_EXPERT = D_MODEL // 2
LANES = 128

VMEM_LIMIT = 56 * 1024 * 1024


def _cparams(sem):
    return pltpu.CompilerParams(dimension_semantics=sem, vmem_limit_bytes=VMEM_LIMIT)


def _dot(a, b):
    return jnp.dot(a, b, preferred_element_type=F32)


def _dot_nt(a, b):
    return lax.dot_general(a, b, (((1,), (1,)), ((), ())), preferred_element_type=F32)


def _dot_tn(a, b):
    return lax.dot_general(a, b, (((0,), (0,)), ((), ())), preferred_element_type=F32)


def _split3(x):
    hi = x.astype(BF16)
    r = x - hi.astype(F32)
    mid = r.astype(BF16)
    lo = (r - mid.astype(F32)).astype(BF16)
    return hi, mid, lo


def _exact_lhs_dot(m_bf16, x):
    hi, mid, lo = _split3(x)
    return _dot(m_bf16, hi) + _dot(m_bf16, mid) + _dot(m_bf16, lo)


def _dot3(a, b):
    a_hi = a.astype(BF16)
    a_lo = (a - a_hi.astype(F32)).astype(BF16)
    b_hi = b.astype(BF16)
    b_lo = (b - b_hi.astype(F32)).astype(BF16)
    return _dot(a_hi, b_hi) + _dot(a_hi, b_lo) + _dot(a_lo, b_hi)


def _sigmoid(x):
    return 1.0 / (1.0 + jnp.exp(-x))


def _silu(x):
    return x * _sigmoid(x)


def _softplus(x):
    return jnp.maximum(x, 0.0) + jnp.log(1.0 + jnp.exp(-jnp.abs(x)))


def _log_sigmoid(x):
    return jnp.minimum(x, 0.0) - jnp.log(1.0 + jnp.exp(-jnp.abs(x)))


_MAIN_WIDTHS = (QK, QK, D_MODEL, D_MODEL, 2 * QK + D_MODEL, D_MODEL, D_MODEL, D_MODEL)
_MAIN_TOTAL = sum(_MAIN_WIDTHS)
_COL_TILE = 512


def _inproj_body(x_ref, g_ref, w_ref, wl_ref, wg_ref, aq, ak, av, ar, bqkv, bg, ma, mb, lr, gates):
    x = x_ref[...]
    ms = jnp.mean(x * x, axis=-1, keepdims=True)
    xn = (x * lax.rsqrt(ms + EPS) * g_ref[...]).astype(BF16)
    col = 0
    for ref in (aq, ak, av, ar, bqkv, bg, ma, mb):
        width = ref.shape[-1]
        for c in range(0, width, _COL_TILE):
            ref[:, c:c + _COL_TILE] = _dot(xn, w_ref[:, col + c:col + c + _COL_TILE]).astype(ref.dtype)
        col += width
    lr[...] = _dot(xn, wl_ref[...])
    gates[...] = _dot(xn, wg_ref[...])


def _pack_in_weights(w_in):
    o = np.cumsum((0, QK, QK, D_MODEL, D_MODEL, GLA_RANK, GLA_RANK,
                   QK, QK, D_MODEL, D_MODEL, N_HEADS, N_HEADS, N_HEADS, N_HEADS, D_MODEL, D_MODEL))
    seg = lambda i: w_in[:, o[i]:o[i + 1]]
    w_main = jnp.concatenate([seg(0), seg(1), seg(2), seg(3), seg(6), seg(7), seg(8), seg(9),
                              seg(14), seg(15)], axis=1).astype(BF16)
    w_lr = jnp.concatenate([seg(4), seg(5), jnp.zeros((D_MODEL, LANES - 2 * GLA_RANK), F32)], axis=1).astype(BF16)
    g4 = jnp.stack([seg(10), seg(11), seg(12), seg(13)], axis=-1)
    w_g = jnp.concatenate([g4, jnp.zeros((D_MODEL, N_HEADS, LANES - 4), F32)], axis=-1)
    w_g = w_g.reshape(D_MODEL, N_HEADS * LANES).astype(BF16)
    return w_main, w_lr, w_g


def _in_proj(x2d, ln_mix, w_main, w_lr, w_g, tm):
    n = x2d.shape[0]
    widths = _MAIN_WIDTHS
    const = lambda i: (0, 0)
    row = lambda i: (i, 0)
    out_shape = [jax.ShapeDtypeStruct((n, w), BF16) for w in widths]
    out_shape += [jax.ShapeDtypeStruct((n, LANES), F32), jax.ShapeDtypeStruct((n, N_HEADS * LANES), F32)]
    out_specs = [pl.BlockSpec((tm, w), row) for w in widths]
    out_specs += [pl.BlockSpec((tm, LANES), row), pl.BlockSpec((tm, N_HEADS * LANES), row)]
    single = pl.Buffered(1)
    return pl.pallas_call(
        _inproj_body,
        grid=(n // tm,),
        in_specs=[pl.BlockSpec((tm, D_MODEL), row),
                  pl.BlockSpec((1, D_MODEL), const),
                  pl.BlockSpec((D_MODEL, _MAIN_TOTAL), const, pipeline_mode=single),
                  pl.BlockSpec((D_MODEL, LANES), const, pipeline_mode=single),
                  pl.BlockSpec((D_MODEL, N_HEADS * LANES), const, pipeline_mode=single)],
        out_specs=out_specs,
        out_shape=out_shape,
        compiler_params=_cparams(("parallel",)),
        name="in_proj",
    )(x2d, ln_mix.reshape(1, D_MODEL), w_main, w_lr, w_g)


def _tri_masks(c):
    r = lax.broadcasted_iota(I32, (c, c), 0)
    s = lax.broadcasted_iota(I32, (c, c), 1)
    return r, s


def _gla_body(qf, kf, vf, lrf, qb, kb, vb, lrb, wf, bf, wb, bb, trif, trib, of, ob, sf, sb):
    @pl.when(pl.program_id(2) == 0)
    def _():
        sf[...] = jnp.zeros_like(sf)
        sb[...] = jnp.zeros_like(sb)

    tb = qf.shape[0]
    c = CHUNK
    nc = tb // c
    r, s = _tri_masks(c)
    streams = []
    for refs, reverse in (((qf, kf, vf, lrf, wf, bf, trif, of, sf), False),
                          ((qb, kb, vb, lrb, wb, bb, trib, ob, sb), True)):
        q_ref, k_ref, v_ref, lr_ref, w_ref, b_ref, tri_ref, o_ref, s_ref = refs
        logits = _dot(lr_ref[...].astype(BF16), w_ref[...]) + b_ref[...]
        gk = _log_sigmoid(logits) * (1.0 / GLA_NORMALIZER)
        cum = _exact_lhs_dot(tri_ref[...], gk)
        streams.append(dict(q=q_ref, k=k_ref, v=v_ref, o=o_ref, s=s_ref, reverse=reverse, cum=cum,
                            incl=(r <= s) if reverse else (r >= s)))
    items = [(st, (nc - 1 - j) if st["reverse"] else j) for j in range(nc) for st in streams]
    work = []
    for st, ci in items:
        r0 = ci * c
        cum = st["cum"][r0:r0 + c]
        edge = r0 if st["reverse"] else r0 + c - 1
        tot = st["cum"][edge:edge + 1]
        q = st["q"][r0:r0 + c, :].astype(F32)
        k = st["k"][r0:r0 + c, :].astype(F32)
        work.append(dict(r0=r0, v=st["v"][r0:r0 + c, :],
                         q_dec=(q * jnp.exp(cum) * (DK ** -0.5)).astype(BF16),
                         k_inv=(k * jnp.exp(-cum)).astype(BF16),
                         k_end_t=jnp.transpose(k * jnp.exp(tot - cum)).astype(BF16),
                         dec=jnp.transpose(jnp.broadcast_to(jnp.exp(tot), (DK, DK)))))
    att = [jnp.where(st["incl"], _dot_nt(y["q_dec"], y["k_inv"]), 0.0).astype(BF16) for (st, _), y in zip(items, work)]
    av = [_dot(jnp.concatenate([a, y["k_end_t"]], axis=0), y["v"]) for a, y in zip(att, work)]
    for (st, ci), y, z in zip(items, work, av):
        r0 = y["r0"]
        s_ref = st["s"]
        state = s_ref[...]
        o = z[0:c] + _dot(y["q_dec"], state.astype(BF16))
        st["o"][r0:r0 + c, :] = o.astype(st["o"].dtype)
        s_ref[:, 0:DK] = state[:, 0:DK] * y["dec"] + z[c:c + DK, 0:DK]
        s_ref[:, DK:DV] = state[:, DK:DV] * y["dec"] + z[c:c + DK, DK:DV]


def _gla(aq, ak, av, lr, w_up, b_up, tb):
    bsz, l, _ = aq.shape
    nb = l // tb
    fwd = lambda b, h, i: (b, i, h)
    bwd = lambda b, h, i: (b, nb - 1 - i, h)
    fwd0 = lambda b, h, i: (b, i, 0)
    bwd0 = lambda b, h, i: (b, nb - 1 - i, 0)
    qk_spec = lambda m: pl.BlockSpec((None, tb, DK), m)
    v_spec = lambda m: pl.BlockSpec((None, tb, DV), m)
    lr_spec = lambda m: pl.BlockSpec((None, tb, LANES), m)
    in_specs = [qk_spec(fwd), qk_spec(fwd), v_spec(fwd), lr_spec(fwd0),
                qk_spec(bwd), qk_spec(bwd), v_spec(bwd), lr_spec(bwd0),
                pl.BlockSpec((LANES, DK), lambda b, h, i: (0, h)),
                pl.BlockSpec((1, DK), lambda b, h, i: (0, h)),
                pl.BlockSpec((LANES, DK), lambda b, h, i: (0, N_HEADS + h)),
                pl.BlockSpec((1, DK), lambda b, h, i: (0, N_HEADS + h)),
                pl.BlockSpec((tb, tb), lambda b, h, i: (0, 0)), pl.BlockSpec((tb, tb), lambda b, h, i: (0, 0))]
    out = jax.ShapeDtypeStruct((bsz, l, D_MODEL), BF16)
    return pl.pallas_call(
        _gla_body,
        grid=(bsz, N_HEADS, nb),
        in_specs=in_specs,
        out_specs=[v_spec(fwd), v_spec(bwd)],
        out_shape=[out, out],
        scratch_shapes=[pltpu.VMEM((DK, DV), F32), pltpu.VMEM((DK, DV), F32)],
        compiler_params=_cparams(("parallel", "parallel", "arbitrary")),
        name="gla",
    )(aq, ak, av, lr, aq, ak, av, lr, w_up, b_up, w_up, b_up, _chunk_tri(tb, False), _chunk_tri(tb, True))


def _pack_gla_up(w_up_f, b_up_f, w_up_b, b_up_b):
    w = jnp.zeros((LANES, 2 * QK), F32)
    w = w.at[0:GLA_RANK, 0:QK].set(w_up_f).at[GLA_RANK:2 * GLA_RANK, QK:2 * QK].set(w_up_b)
    b = jnp.concatenate([b_up_f, b_up_b]).reshape(1, 2 * QK).astype(F32)
    return w.astype(BF16), b


_HALO = 16


def _gdn_prep_body(x_ref, prev_ref, next_ref, w_ref, q_ref, k_ref, v_ref):
    i = pl.program_id(1)
    nb = pl.num_programs(1)
    tb = x_ref.shape[0]
    x = x_ref[...].astype(F32)
    prev_row = jnp.where(i > 0, prev_ref[_HALO - 1:_HALO, :].astype(F32), 0.0)
    next_row = jnp.where(i < nb - 1, next_ref[0:1, :].astype(F32), 0.0)
    ridx = lax.broadcasted_iota(I32, (tb, 1), 0)
    x_prev = jnp.where(ridx == 0, prev_row, pltpu.roll(x, 1, axis=0))
    x_next = jnp.where(ridx == tb - 1, next_row, pltpu.roll(x, tb - 1, axis=0))
    y = _silu(x_prev * w_ref[0:1, :] + x * w_ref[1:2, :] + x_next * w_ref[2:3, :])
    for h in range(N_HEADS):
        qh = y[:, h * DK:(h + 1) * DK]
        q_ref[:, h * DK:(h + 1) * DK] = (
            qh * (lax.rsqrt(jnp.sum(qh * qh, axis=-1, keepdims=True) + EPS) * DK ** -0.5)).astype(q_ref.dtype)
        kh = y[:, QK + h * DK:QK + (h + 1) * DK]
        k_ref[:, h * DK:(h + 1) * DK] = (
            kh * lax.rsqrt(jnp.sum(kh * kh, axis=-1, keepdims=True) + EPS)).astype(k_ref.dtype)
    v_ref[...] = y[:, 2 * QK:].astype(v_ref.dtype)


def _gdn_prep(bqkv, conv_w, tb):
    bsz, l, width = bqkv.shape
    nb = l // tb
    hb = tb // _HALO
    nh = l // _HALO
    return pl.pallas_call(
        _gdn_prep_body,
        grid=(bsz, nb),
        in_specs=[pl.BlockSpec((None, tb, width), lambda b, i: (b, i, 0)),
                  pl.BlockSpec((None, _HALO, width), lambda b, i: (b, jnp.maximum(i * hb - 1, 0), 0)),
                  pl.BlockSpec((None, _HALO, width), lambda b, i: (b, jnp.minimum((i + 1) * hb, nh - 1), 0)),
                  pl.BlockSpec((3, width), lambda b, i: (0, 0))],
        out_specs=[pl.BlockSpec((None, tb, QK), lambda b, i: (b, i, 0)),
                   pl.BlockSpec((None, tb, QK), lambda b, i: (b, i, 0)),
                   pl.BlockSpec((None, tb, D_MODEL), lambda b, i: (b, i, 0))],
        out_shape=[jax.ShapeDtypeStruct((bsz, l, QK), BF16), jax.ShapeDtypeStruct((bsz, l, QK), BF16),
                   jax.ShapeDtypeStruct((bsz, l, D_MODEL), BF16)],
        compiler_params=_cparams(("parallel", "parallel")),
        name="gdn_prep",
    )(bqkv, bqkv, bqkv, conv_w.astype(F32))


_G_BETA = 0
_G_A = 2


def _pack_gdn_params(a_log_f, dt_bias_f, a_log_b, dt_bias_b):
    p = jnp.zeros((8, N_HEADS, LANES), F32)
    p = p.at[0, :, _G_A].set(-jnp.exp(a_log_f.astype(F32))).at[0, :, _G_A + 1].set(-jnp.exp(a_log_b.astype(F32)))
    p = p.at[1, :, _G_A].set(dt_bias_f.astype(F32)).at[1, :, _G_A + 1].set(dt_bias_b.astype(F32))
    return p.reshape(8, N_HEADS * LANES)


def _each(fn, *lists):
    out = []
    for args in zip(*lists):
        out.append(fn(*args))
        yield
    return out


def _interleave(main, side, side_per_main):
    main_live = side_live = True
    while main_live or side_live:
        if main_live:
            main_live = next(main, "done") != "done"
        for _ in range(side_per_main if main_live else 1):
            if side_live:
                side_live = next(side, "done") != "done"


def _unit_tri_inverse_many(a_list, r, s, eye):
    c = CHUNK
    bf = lambda x: x.astype(BF16)
    same16 = (r >> 4) == (s >> 4)
    same32 = (r >> 5) == (s >> 5)
    off16 = jnp.logical_and(same32, jnp.logical_not(same16))
    d = [jnp.where(same16, a, 0.0) for a in a_list]
    n = [eye - x for x in d]
    p = yield from _each(lambda x: _dot(bf(x), bf(x)), d)
    for _ in range(2):
        res = yield from _each(lambda x, y: _dot(jnp.concatenate([bf(x), bf(y)], axis=0), bf(y)), n, p)
        n = [x + z[0:c] for x, z in zip(n, res)]
        p = [z[c:2 * c] for z in res]
    n = yield from _each(lambda x, y: x + _dot(bf(x), bf(y)), n, p)
    for e in ([jnp.where(off16, a, 0.0) for a in a_list], [jnp.where(same32, 0.0, a) for a in a_list]):
        t = yield from _each(lambda x, y: _dot(bf(x), bf(y)), n, e)
        n = yield from _each(lambda x, y: x - _dot(bf(y), bf(x)), n, t)
    return n


_GDN_HEADS_PER_STEP = 2


def _gdn_body(qf, kf, vf, gf, qb, kb, vb, gb, p_ref, trif, trib, of, ob, s_all, u_s, wq_s, att_s, ke_s, dec_s):
    step = pl.program_id(2)

    @pl.when(step == 0)
    def _():
        s_all[...] = jnp.zeros_like(s_all)
        for ref in (u_s, wq_s, att_s, ke_s, dec_s):
            ref[1] = jnp.zeros(ref.shape[1:], ref.dtype)

    tb = qf.shape[0]
    c = CHUNK
    nc = tb // c
    rd = (step + 1) % 2
    wr = step % 2
    order = [(hh, d, (nc - 1 - j) if d else j) for j in range(nc) for hh in range(_GDN_HEADS_PER_STEP) for d in (0, 1)]

    def recurrence():
        for idx, (hh, d, ci) in enumerate(order):
            r0 = ci * c
            o_ref = ob if d else of
            state = s_all[2 * hh + d]
            ws = _dot(wq_s[rd, idx], state.astype(BF16))
            v_new = (u_s[rd, idx] - ws[0:c]).astype(BF16)
            av = _dot(jnp.concatenate([att_s[rd, idx], ke_s[rd, idx]], axis=0), v_new)
            o = ws[c:2 * c] + av[0:c]
            o_ref[r0:r0 + c, hh * DV:(hh + 1) * DV] = o.astype(o_ref.dtype)
            s_all[2 * hh + d] = state * dec_s[rd, idx][0:1, 0:1] + av[c:c + DK]
            yield

    _interleave(recurrence(), _gdn_prepare(qf, kf, vf, gf, qb, kb, vb, gb, p_ref, trif, trib, order, wr,
                                           u_s, wq_s, att_s, ke_s, dec_s), side_per_main=_GDN_PREP_PER_REC)


_GDN_PREP_PER_REC = 11


def _gdn_prepare(qf, kf, vf, gf, qb, kb, vb, gb, p_ref, trif, trib, order, wr, u_s, wq_s, att_s, ke_s, dec_s):
    c = CHUNK
    r, s = _tri_masks(c)
    eye = jnp.where(r == s, 1.0, 0.0)
    incl = (r >= s, r <= s)
    strict = (r > s, r < s)
    streams = {}
    for hh in range(_GDN_HEADS_PER_STEP):
        lanes = slice(hh * LANES, (hh + 1) * LANES)
        for refs, reverse in (((qf, kf, vf, gf, trif), False), ((qb, kb, vb, gb, trib), True)):
            q_ref, k_ref, v_ref, g_ref, tri_ref = refs
            gates = g_ref[:, lanes]
            g_all = p_ref[0:1, lanes] * _softplus(gates + p_ref[1:2, lanes])
            cum_all = _exact_lhs_dot(tri_ref[...], g_all)
            streams[hh, int(reverse)] = dict(
                q=q_ref, k=k_ref, v=v_ref, reverse=reverse, hh=hh,
                beta=_sigmoid(gates), cum=cum_all, cum_t=jnp.transpose(cum_all),
                la=_G_A + int(reverse), lb=_G_BETA + int(reverse))

    def chunk_inputs(item):
        hh, d, ci = item
        st = streams[hh, d]
        r0 = ci * c
        la, lb = st["la"], st["lb"]
        edge = r0 if st["reverse"] else r0 + c - 1
        qk_cols = slice(hh * DK, (hh + 1) * DK)
        ccol = st["cum"][r0:r0 + c, la:la + 1]
        crow = st["cum_t"][la:la + 1, r0:r0 + c]
        return dict(ccol=ccol, d=d, r0=r0, st=st,
                    tot=st["cum"][edge:edge + 1, la:la + 1],
                    beta=st["beta"][r0:r0 + c, lb:lb + 1],
                    decay=jnp.where(incl[d], jnp.exp(jnp.where(incl[d], ccol - crow, 0.0)), 0.0),
                    qb=st["q"][r0:r0 + c, qk_cols], kb=st["k"][r0:r0 + c, qk_cols])

    pre = yield from _each(chunk_inputs, order)
    kq = yield from _each(lambda x: _dot_nt(jnp.concatenate([x["kb"], x["qb"]], axis=0), x["kb"]), pre)
    a_list = [jnp.where(strict[x["d"]], y[0:c] * x["beta"] * x["decay"], 0.0) for x, y in zip(pre, kq)]
    t_inv = yield from _unit_tri_inverse_many(a_list, r, s, eye)

    def stage(idx, x, t, y):
        r0 = x["r0"]
        k = x["kb"].astype(F32)
        e_cum = jnp.exp(x["ccol"])
        v_cols = slice(x["st"]["hh"] * DV, (x["st"]["hh"] + 1) * DV)
        rhs = jnp.concatenate([(x["st"]["v"][r0:r0 + c, v_cols].astype(F32) * x["beta"]).astype(BF16),
                               (k * (x["beta"] * e_cum)).astype(BF16)], axis=1)
        uw = _dot(t.astype(BF16), rhs)
        u_s[wr, idx] = uw[:, 0:DV]
        wq_s[wr, idx] = jnp.concatenate([uw[:, DV:DV + DK].astype(BF16),
                                         (x["qb"].astype(F32) * e_cum).astype(BF16)], axis=0)
        att_s[wr, idx] = (y[c:2 * c] * x["decay"]).astype(BF16)
        ke_s[wr, idx] = jnp.transpose(k * jnp.exp(x["tot"] - x["ccol"])).astype(BF16)
        dec_s[wr, idx] = jnp.broadcast_to(jnp.exp(x["tot"]), dec_s.shape[2:])

    yield from _each(stage, range(len(pre)), pre, t_inv, kq)


def _chunk_tri(tb, reverse):
    i = np.arange(tb)
    same = (i[:, None] // CHUNK) == (i[None, :] // CHUNK)
    tri = (i[:, None] <= i[None, :]) if reverse else (i[:, None] >= i[None, :])
    return jnp.asarray(np.where(same & tri, 1.0, 0.0), BF16)


def _gdn(cq, ck, cv, gates, gparams, tb):
    bsz, l, _ = cq.shape
    nb = l // tb
    hps = _GDN_HEADS_PER_STEP
    n_items = 2 * hps * (tb // CHUNK)
    fwd = lambda b, h, i: (b, jnp.minimum(i, nb - 1), h)
    bwd = lambda b, h, i: (b, nb - 1 - jnp.minimum(i, nb - 1), h)
    fwd_o = lambda b, h, i: (b, jnp.maximum(i - 1, 0), h)
    bwd_o = lambda b, h, i: (b, nb - 1 - jnp.maximum(i - 1, 0), h)
    qk_spec = lambda m: pl.BlockSpec((None, tb, hps * DK), m)
    v_spec = lambda m: pl.BlockSpec((None, tb, hps * DV), m)
    g_spec = lambda m: pl.BlockSpec((None, tb, hps * LANES), m)
    const = lambda b, h, i: (0, 0)
    in_specs = [qk_spec(fwd), qk_spec(fwd), v_spec(fwd), g_spec(fwd),
                qk_spec(bwd), qk_spec(bwd), v_spec(bwd), g_spec(bwd),
                pl.BlockSpec((8, hps * LANES), lambda b, h, i: (0, h)),
                pl.BlockSpec((tb, tb), const), pl.BlockSpec((tb, tb), const)]
    out = jax.ShapeDtypeStruct((bsz, l, D_MODEL), BF16)
    return pl.pallas_call(
        _gdn_body,
        grid=(bsz, N_HEADS // hps, nb + 1),
        in_specs=in_specs,
        out_specs=[v_spec(fwd_o), v_spec(bwd_o)],
        out_shape=[out, out],
        scratch_shapes=[pltpu.VMEM((2 * hps, DK, DV), F32),
                        pltpu.VMEM((2, n_items, CHUNK, DV), F32),
                        pltpu.VMEM((2, n_items, 2 * CHUNK, DK), BF16),
                        pltpu.VMEM((2, n_items, CHUNK, CHUNK), BF16),
                        pltpu.VMEM((2, n_items, DK, CHUNK), BF16),
                        pltpu.VMEM((2, n_items, 8, LANES), F32)],
        compiler_params=_cparams(("parallel", "parallel", "arbitrary")),
        name="gdn",
    )(cq, ck, cv, gates, cq, ck, cv, gates, gparams, _chunk_tri(tb, False), _chunk_tri(tb, True))


def _rmsnorm(x, w):
    return x * lax.rsqrt(jnp.mean(x * x, axis=-1, keepdims=True) + EPS) * w


def _mem_kv_body(m_ref, g_ref, wk_ref, wv_ref, k_ref, v_ref):
    mn = _rmsnorm(m_ref[...], g_ref[...]).astype(BF16)
    k_ref[...] = _dot(mn, wk_ref[...]).astype(k_ref.dtype)
    v_ref[...] = _dot(mn, wv_ref[...]).astype(v_ref.dtype)


def _mem_kv(mem, ln_mem, wk, wv):
    bsz, m, _ = mem.shape
    const = lambda b: (0, 0)
    blk = pl.BlockSpec((None, m, D_MODEL), lambda b: (b, 0, 0))
    out = jax.ShapeDtypeStruct((bsz, m, D_MODEL), BF16)
    return pl.pallas_call(
        _mem_kv_body,
        grid=(bsz,),
        in_specs=[blk, pl.BlockSpec((1, D_MODEL), const),
                  pl.BlockSpec((D_MODEL, D_MODEL), const), pl.BlockSpec((D_MODEL, D_MODEL), const)],
        out_specs=[blk, blk],
        out_shape=[out, out],
        compiler_params=_cparams(("parallel",)),
        name="mem_kv",
    )(mem, ln_mem.reshape(1, D_MODEL), wk, wv)


_R_E1, _R_E2, _R_W1, _R_W2 = 0, 1, 2, 3


def _head_rmsnorm(o, w):
    parts = []
    for h in range(N_HEADS):
        oh = o[:, h * DV:(h + 1) * DV]
        parts.append(oh * lax.rsqrt(jnp.mean(oh * oh, axis=-1, keepdims=True) + EPS))
    return jnp.concatenate(parts, axis=-1) * w


def _route_tile(lg):
    neg = -1e30
    big = 1e9
    lane = lax.broadcasted_iota(I32, lg.shape, 1).astype(F32)
    gmask = lane < N_GROUPS
    gl = jnp.where(gmask, lg, neg)
    gmax = jnp.max(gl, axis=-1, keepdims=True)
    gidx = jnp.min(jnp.where(gl == gmax, lane, big), axis=-1, keepdims=True)
    gsum = jnp.sum(jnp.where(gmask, jnp.exp(gl - gmax), 0.0), axis=-1, keepdims=True)
    lo = N_GROUPS + EXP_PER_GROUP * gidx
    off = lane - lo
    el = jnp.where(jnp.abs(off - (EXP_PER_GROUP - 1) * 0.5) < EXP_PER_GROUP * 0.5, lg, neg)
    m1 = jnp.max(el, axis=-1, keepdims=True)
    i1 = jnp.min(jnp.where(el == m1, lane, big), axis=-1, keepdims=True)
    el2 = jnp.where(lane == i1, neg, el)
    m2 = jnp.max(el2, axis=-1, keepdims=True)
    i2 = jnp.min(jnp.where(el2 == m2, lane, big), axis=-1, keepdims=True)
    r = jnp.exp(m2 - m1)
    p1 = 1.0 / (1.0 + r)
    gw = 1.0 / gsum
    w1 = gw * p1
    w2 = gw * (r * p1)
    e1 = i1 - N_GROUPS
    e2 = i2 - N_GROUPS
    return jnp.where(lane == _R_E1, e1, jnp.where(lane == _R_E2, e2,
                     jnp.where(lane == _R_W1, w1, jnp.where(lane == _R_W2, w2, 0.0))))


def _post_body(x_ref, oaf, oab, obf, obb, ar, bg, ma, mb, na_ref, nb_ref, wout, lnxa, wq, kmem, vmem, wo, lnmoe,
               wrh, wrl, br, h2_ref, hn2_ref, route_ref):
    f = lambda ref: ref[...].astype(F32)
    oa = _head_rmsnorm(f(oaf) + f(oab), na_ref[...]) * _silu(f(ar))
    ob = _head_rmsnorm(f(obf) + f(obb), nb_ref[...]) * _silu(f(bg))
    y = _sigmoid(f(ma)) * oa + _sigmoid(f(mb)) * ob
    h1 = x_ref[...] + _dot(y.astype(BF16), wout[...])
    hn = _rmsnorm(h1, lnxa[...]).astype(BF16)
    q = _dot(hn, wq[...])
    outs = []
    for h in range(N_HEADS):
        sl = slice(h * XA_DH, (h + 1) * XA_DH)
        s = _dot_nt(q[:, sl].astype(BF16), kmem[:, sl]) * (XA_DH ** -0.5)
        p = jnp.exp(s - jnp.max(s, axis=-1, keepdims=True))
        denom = jnp.sum(p, axis=-1, keepdims=True)
        outs.append(_dot(p.astype(BF16), vmem[:, sl]) * (1.0 / denom))
    o = jnp.concatenate(outs, axis=-1)
    h2 = h1 + _dot(o.astype(BF16), wo[...])
    h2_ref[...] = h2
    hn2 = _rmsnorm(h2, lnmoe[...])
    hn2_ref[...] = hn2
    x_hi = hn2.astype(BF16)
    x_lo = (hn2 - x_hi.astype(F32)).astype(BF16)
    lg = _dot(x_hi, wrh[...]) + _dot(x_hi, wrl[...]) + _dot(x_lo, wrh[...]) + br[...]
    route_ref[...] = _route_tile(lg)


def _post(x, oaf, oab, obf, obb, ar, bg, ma, mb, kmem, vmem, p, tm):
    bsz, l, _ = x.shape
    nb = l // tm
    blk = pl.BlockSpec((None, tm, D_MODEL), lambda b, i: (b, i, 0))
    const = lambda b, i: (0, 0)
    vec = pl.BlockSpec((1, D_MODEL), const)
    mat = pl.BlockSpec((D_MODEL, D_MODEL), const, pipeline_mode=pl.Buffered(1))
    memb = pl.BlockSpec((None, kmem.shape[1], D_MODEL), lambda b, i: (b, 0, 0))
    rmat = pl.BlockSpec((D_MODEL, LANES), const)
    return pl.pallas_call(
        _post_body,
        grid=(bsz, nb),
        in_specs=[blk] * 9 + [vec, vec, mat, vec, mat, memb, memb, mat, vec, rmat, rmat,
                               pl.BlockSpec((1, LANES), const)],
        out_specs=[blk, blk, pl.BlockSpec((None, tm, LANES), lambda b, i: (b, i, 0))],
        out_shape=[jax.ShapeDtypeStruct((bsz, l, D_MODEL), F32), jax.ShapeDtypeStruct((bsz, l, D_MODEL), F32),
                   jax.ShapeDtypeStruct((bsz, l, LANES), F32)],
        compiler_params=_cparams(("parallel", "parallel")),
        name="post",
    )(x, oaf, oab, obf, obb, ar, bg, ma, mb, p["gla_onorm"], p["gdn_onorm"], p["w_out"], p["ln_xa"], p["xa_wq"],
      kmem, vmem, p["xa_wo"], p["ln_moe"], p["w_r_hi"], p["w_r_lo"], p["b_r"])


def _route_body(rt_ref, low_ref, rank_ref, cnt_ref, carry):
    @pl.when(pl.program_id(0) == 0)
    def _():
        carry[...] = jnp.zeros_like(carry)

    rt = rt_ref[...]
    lane = lax.broadcasted_iota(I32, rt.shape, 1).astype(F32)
    oh1 = jnp.where(lane == rt[:, _R_E1:_R_E1 + 1], 1.0, 0.0)
    oh2 = jnp.where(lane == rt[:, _R_E2:_R_E2 + 1], 1.0, 0.0)
    tot = oh1 + oh2
    before = _dot(low_ref[...], tot.astype(BF16)) + carry[0:1, :]
    r1 = jnp.sum(before * oh1, axis=-1, keepdims=True)
    r2 = jnp.sum(before * oh2, axis=-1, keepdims=True)
    rank_ref[...] = jnp.where(lane == 0, r1, jnp.where(lane == 1, r2, 0.0))
    new = carry[0:1, :] + jnp.sum(tot, axis=0, keepdims=True)
    carry[...] = jnp.broadcast_to(new, carry.shape)
    cnt_ref[...] = jnp.broadcast_to(new, cnt_ref.shape)


def _route(route2d, tr):
    n = route2d.shape[0]
    i = np.arange(tr)
    low = jnp.asarray(np.where(i[:, None] > i[None, :], 1.0, 0.0), BF16)
    return pl.pallas_call(
        _route_body,
        grid=(n // tr,),
        in_specs=[pl.BlockSpec((tr, LANES), lambda i: (i, 0)), pl.BlockSpec((tr, tr), lambda i: (0, 0))],
        out_specs=[pl.BlockSpec((tr, LANES), lambda i: (i, 0)), pl.BlockSpec((8, LANES), lambda i: (0, 0))],
        out_shape=[jax.ShapeDtypeStruct((n, LANES), F32), jax.ShapeDtypeStruct((8, LANES), F32)],
        scratch_shapes=[pltpu.VMEM((8, LANES), F32)],
        compiler_params=_cparams(("arbitrary",)),
        name="route",
    )(route2d, low)


_ROW_UNROLL = 8


def _scatter_body(dest_ref, hn_ref, xs_in, xs_out, sem):
    del xs_in
    ts = hn_ref.shape[0]

    def row_copy(t, d):
        return pltpu.make_async_copy(hn_ref.at[pl.ds(t, 1)], xs_out.at[pl.ds(d, 1)], sem)

    def issue(t, carry):
        row_copy(t, dest_ref[0, t]).start()
        row_copy(t, dest_ref[1, t]).start()
        return carry

    lax.fori_loop(0, ts, issue, 0, unroll=_ROW_UNROLL)

    def drain(t, carry):
        row_copy(0, 0).wait()
        row_copy(0, 0).wait()
        return carry

    lax.fori_loop(0, ts, drain, 0, unroll=_ROW_UNROLL)


def _scatter(dest, hn2d, xs_init, ts):
    n = hn2d.shape[0]
    return pl.pallas_call(
        _scatter_body,
        grid=(n // ts,),
        in_specs=[pl.BlockSpec((2, ts), lambda i: (0, i), memory_space=pltpu.SMEM),
                  pl.BlockSpec((ts, D_MODEL), lambda i: (i, 0)),
                  pl.BlockSpec(memory_space=pl.ANY)],
        out_specs=pl.BlockSpec(memory_space=pl.ANY),
        out_shape=jax.ShapeDtypeStruct(xs_init.shape, xs_init.dtype),
        scratch_shapes=[pltpu.SemaphoreType.DMA(())],
        input_output_aliases={2: 0},
        compiler_params=_cparams(("arbitrary",)),
        name="scatter",
    )(dest, hn2d, xs_init)


_FFN_BLOCK = 512


def _ffn_body(be_ref, nu_ref, xs_ref, wg_ref, wu_ref, wd_ref, ys_ref, wg_s, wu_s, wd_s):
    i = pl.program_id(0)
    fresh = jnp.logical_or(i == 0, be_ref[i] != be_ref[jnp.maximum(i - 1, 0)])

    @pl.when(fresh)
    def _():
        wg_s[...] = wg_ref[...].astype(BF16)
        wu_s[...] = wu_ref[...].astype(BF16)
        wd_s[...] = wd_ref[...].astype(BF16)

    @pl.when(i < nu_ref[0])
    def _():
        x = xs_ref[...].astype(BF16)
        hidden = _silu(_dot(x, wg_s[...])) * _dot(x, wu_s[...])
        ys_ref[...] = _dot(hidden.astype(BF16), wd_s[...])

    @pl.when(i >= nu_ref[0])
    def _():
        ys_ref[...] = jnp.zeros_like(ys_ref)


def _ffn(block_e, n_used, xs, w_gate, w_up, w_down):
    p_len = xs.shape[0]
    nblk = p_len // _FFN_BLOCK
    grid_spec = pltpu.PrefetchScalarGridSpec(
        num_scalar_prefetch=2,
        grid=(nblk,),
        in_specs=[pl.BlockSpec((_FFN_BLOCK, D_MODEL), lambda i, be, nu: (i, 0)),
                  pl.BlockSpec((None, D_MODEL, D_EXPERT), lambda i, be, nu: (be[i], 0, 0)),
                  pl.BlockSpec((None, D_MODEL, D_EXPERT), lambda i, be, nu: (be[i], 0, 0)),
                  pl.BlockSpec((None, D_EXPERT, D_MODEL), lambda i, be, nu: (be[i], 0, 0))],
        out_specs=pl.BlockSpec((_FFN_BLOCK, D_MODEL), lambda i, be, nu: (i, 0)),
        scratch_shapes=[pltpu.VMEM((D_MODEL, D_EXPERT), BF16), pltpu.VMEM((D_MODEL, D_EXPERT), BF16),
                        pltpu.VMEM((D_EXPERT, D_MODEL), BF16)])
    return pl.pallas_call(
        _ffn_body,
        grid_spec=grid_spec,
        out_shape=jax.ShapeDtypeStruct((p_len, D_MODEL), F32),
        compiler_params=_cparams(("arbitrary",)),
        name="ffn",
    )(block_e, n_used, xs, w_gate, w_up, w_down)


def _combine_body(dest_ref, h2_ref, rt_ref, ys_ref, g_ref, out_ref, buf, sem):
    ts = h2_ref.shape[0]

    def row_copy(k, t, d):
        return pltpu.make_async_copy(ys_ref.at[pl.ds(d, 1)], buf.at[k, pl.ds(t, 1)], sem)

    def issue(t, carry):
        row_copy(0, t, dest_ref[0, t]).start()
        row_copy(1, t, dest_ref[1, t]).start()
        return carry

    lax.fori_loop(0, ts, issue, 0, unroll=_ROW_UNROLL)

    def drain(t, carry):
        row_copy(0, 0, 0).wait()
        row_copy(1, 0, 0).wait()
        return carry

    lax.fori_loop(0, ts, drain, 0, unroll=_ROW_UNROLL)
    rt = rt_ref[...]
    w1 = rt[:, _R_W1:_R_W1 + 1]
    w2 = rt[:, _R_W2:_R_W2 + 1]
    h3 = h2_ref[...] + w1 * buf[0] + w2 * buf[1]
    out_ref[...] = _rmsnorm(h3, g_ref[...])


def _combine(dest, h2, route2d, ys, ln_final, ts):
    n = h2.shape[0]
    return pl.pallas_call(
        _combine_body,
        grid=(n // ts,),
        in_specs=[pl.BlockSpec((2, ts), lambda i: (0, i), memory_space=pltpu.SMEM),
                  pl.BlockSpec((ts, D_MODEL), lambda i: (i, 0)),
                  pl.BlockSpec((ts, LANES), lambda i: (i, 0)),
                  pl.BlockSpec(memory_space=pl.ANY),
                  pl.BlockSpec((1, D_MODEL), lambda i: (0, 0))],
        out_specs=pl.BlockSpec((ts, D_MODEL), lambda i: (i, 0)),
        out_shape=jax.ShapeDtypeStruct((n, D_MODEL), F32),
        scratch_shapes=[pltpu.VMEM((2, ts, D_MODEL), F32), pltpu.SemaphoreType.DMA(())],
        compiler_params=_cparams(("arbitrary",)),
        name="combine",
    )(dest, h2, route2d, ys, ln_final.reshape(1, D_MODEL))


def _moe(h2, hn2, route2d, p):
    n = h2.shape[0]
    rank, counts = _route(route2d, tr=min(512, n))
    counts = counts[0, :N_EXPERTS].astype(I32)
    blk = _FFN_BLOCK
    padded = (counts + blk - 1) // blk * blk
    pends = jnp.cumsum(padded)
    pstart = pends - padded
    nblk = 2 * n // blk + N_EXPERTS
    starts = jnp.arange(nblk, dtype=I32) * blk
    block_e = jnp.minimum(jnp.sum((pends[None, :] <= starts[:, None]).astype(I32), axis=1), N_EXPERTS - 1)
    n_used = (pends[-1:] // blk).astype(I32)
    eid = route2d[:, _R_E1:_R_E2 + 1].astype(I32)
    dest = (pstart[eid] + rank[:, 0:2].astype(I32)).T
    xs = _scatter(dest, hn2, jnp.zeros((nblk * blk, D_MODEL), F32), ts=256)
    ys = _ffn(block_e, n_used, xs, p["moe_w_gate"], p["moe_w_up"], p["moe_w_down"])
    return _combine(dest, h2, route2d, ys, p["ln_final"], ts=256)


def _encode_group(x, mem, p):
    bsz, l, _ = x.shape
    n = bsz * l
    r3 = lambda t: t.reshape(bsz, l, t.shape[-1])
    aq, ak, av, ar, bqkv, bg, ma, mb, lr, gates = _in_proj(
        x.reshape(n, D_MODEL), p["ln_mix"], p["w_main"], p["w_lr"], p["w_g"], tm=min(512, n))
    cq, ck, cv = _gdn_prep(r3(bqkv), p["gdn_conv"], tb=min(512, l))
    oaf, oab = _gla(r3(aq), r3(ak), r3(av), r3(lr), p["gla_w_up"], p["gla_b_up"], tb=min(512, l))
    obf, obb = _gdn(cq, ck, cv, r3(gates), p["gdn_params"], tb=min(512, l))
    kmem, vmem = _mem_kv(mem, p["ln_mem"], p["xa_wk"], p["xa_wv"])
    h2, hn2, route = _post(x, oaf, oab, obf, obb, r3(ar), r3(bg), r3(ma), r3(mb), kmem, vmem, p, tm=min(512, l))
    out = _moe(h2.reshape(n, D_MODEL), hn2.reshape(n, D_MODEL), route.reshape(n, LANES), p)
    return out.reshape(bsz, l, D_MODEL)


def kernel(x_prompt, x_sample, mem_prompt, mem_sample, ln_mix, w_in, gla_w_up_f, gla_b_up_f, gla_w_up_b, gla_b_up_b, gla_onorm, gdn_conv, gdn_a_log_f, gdn_dt_bias_f, gdn_a_log_b, gdn_dt_bias_b, gdn_onorm, w_out, ln_xa, ln_mem, xa_wq, xa_wk, xa_wv, xa_wo, ln_moe, moe_w_group, moe_b_group, moe_w_expert, moe_b_expert, moe_w_gate, moe_w_up, moe_w_down, ln_final):
    w_main, w_lr, w_g = _pack_in_weights(w_in[0])
    gla_w_up, gla_b_up = _pack_gla_up(gla_w_up_f[0], gla_b_up_f[0], gla_w_up_b[0], gla_b_up_b[0])
    vec = lambda t: t.reshape(1, D_MODEL).astype(F32)
    w_r = jnp.concatenate([moe_w_group[0], moe_w_expert[0],
                           jnp.zeros((D_MODEL, LANES - N_GROUPS - N_EXPERTS), F32)], axis=1)
    w_r_hi = w_r.astype(BF16)
    b_r = jnp.concatenate([moe_b_group[0], moe_b_expert[0], jnp.zeros((LANES - N_GROUPS - N_EXPERTS,), F32)])
    p = dict(ln_mix=ln_mix[0], w_main=w_main, w_lr=w_lr, w_g=w_g, gla_w_up=gla_w_up, gla_b_up=gla_b_up,
             gdn_conv=gdn_conv[0],
             gdn_params=_pack_gdn_params(gdn_a_log_f[0], gdn_dt_bias_f[0], gdn_a_log_b[0], gdn_dt_bias_b[0]),
             gla_onorm=vec(jnp.tile(gla_onorm[0], N_HEADS)), gdn_onorm=vec(jnp.tile(gdn_onorm[0], N_HEADS)),
             w_out=w_out[0].astype(BF16), ln_xa=vec(ln_xa[0]), ln_mem=ln_mem[0],
             xa_wq=xa_wq[0].astype(BF16), xa_wk=xa_wk[0].astype(BF16), xa_wv=xa_wv[0].astype(BF16),
             xa_wo=xa_wo[0].astype(BF16), ln_moe=vec(ln_moe[0]),
             w_r_hi=w_r_hi, w_r_lo=(w_r - w_r_hi.astype(F32)).astype(BF16), b_r=b_r.reshape(1, LANES),
             moe_w_gate=moe_w_gate[0], moe_w_up=moe_w_up[0], moe_w_down=moe_w_down[0], ln_final=ln_final)
    return (_encode_group(x_prompt, mem_prompt, p), _encode_group(x_sample, mem_sample, p))
```

```python
import functools
import math

import jax
import jax.numpy as jnp
import numpy as np
from jax import lax
from jax.experimental import pallas as pl
from jax.experimental.pallas import tpu as pltpu

F32 = jnp.float32
BF16 = jnp.bfloat16
I32 = jnp.int32

D_MODEL = 1024
EPS = 1e-6
N_HEADS = 4
DK = 128
DV = 256
QK = N_HEADS * DK
GLA_RANK = 16
GLA_NORMALIZER = 16.0
CHUNK = 64
XA_DH = D_MODEL // N_HEADS
N_GROUPS = 4
EXP_PER_GROUP = 8
N_EXPERTS = N_GROUPS * EXP_PER_GROUP
D_EXPERT = D_MODEL // 2
LANES = 128
_GDN_HEADS_PER_STEP = 2
_GDN_GROUPS = N_HEADS // _GDN_HEADS_PER_STEP

VMEM_LIMIT = 56 * 1024 * 1024


def _cparams(sem):
    return pltpu.CompilerParams(dimension_semantics=sem, vmem_limit_bytes=VMEM_LIMIT)


def _dot(a, b):
    return jnp.dot(a, b, preferred_element_type=F32)


def _dot_nt(a, b):
    return lax.dot_general(a, b, (((1,), (1,)), ((), ())), preferred_element_type=F32)


def _dot_tn(a, b):
    return lax.dot_general(a, b, (((0,), (0,)), ((), ())), preferred_element_type=F32)


def _split3(x):
    hi = x.astype(BF16)
    r = x - hi.astype(F32)
    mid = r.astype(BF16)
    lo = (r - mid.astype(F32)).astype(BF16)
    return hi, mid, lo


def _chunk_cumsum(tri3, x):
    hi, mid, lo = _split3(x)
    outs = []
    for r0 in range(0, x.shape[0], CHUNK):
        rows = slice(r0, r0 + CHUNK)
        outs.append(_dot(tri3, jnp.concatenate([hi[rows], mid[rows], lo[rows]], axis=0)))
    return jnp.concatenate(outs, axis=0)


def _dot3(a, b):
    a_hi = a.astype(BF16)
    a_lo = (a - a_hi.astype(F32)).astype(BF16)
    b_hi = b.astype(BF16)
    b_lo = (b - b_hi.astype(F32)).astype(BF16)
    return _dot(a_hi, b_hi) + _dot(a_hi, b_lo) + _dot(a_lo, b_hi)


def _sigmoid(x):
    return 0.5 * jnp.tanh(0.5 * x) + 0.5


def _silu(x):
    return x * _sigmoid(x)


def _softplus(x):
    return jnp.maximum(x, 0.0) + jnp.log(1.0 + jnp.exp(-jnp.abs(x)))


def _log_sigmoid(x):
    return jnp.minimum(x, 0.0) - jnp.log(1.0 + jnp.exp(-jnp.abs(x)))


_MAIN_WIDTHS = (QK, QK, D_MODEL, D_MODEL, 2 * QK + D_MODEL, D_MODEL, D_MODEL, D_MODEL)
_MAIN_TOTAL = sum(_MAIN_WIDTHS)
_COL_TILE = 512


def _inproj_body(x_ref, g_ref, w_ref, wl_ref, wg_ref, aq, ak, av, ar, bqkv, bg, ma, mb, lr, gates):
    x = x_ref[...]
    ms = jnp.mean(x * x, axis=-1, keepdims=True)
    xn = (x * lax.rsqrt(ms + EPS) * g_ref[...]).astype(BF16)
    col = 0
    for ref in (aq, ak, av, ar, bqkv, bg, ma, mb):
        width = ref.shape[-1]
        for c in range(0, width, _COL_TILE):
            ref[:, c:c + _COL_TILE] = _dot(xn, w_ref[:, col + c:col + c + _COL_TILE]).astype(ref.dtype)
        col += width
    lr[...] = _dot(xn, wl_ref[...])
    gates[...] = _dot(xn, wg_ref[...])


def _pack_in_weights(w_in):
    o = np.cumsum((0, QK, QK, D_MODEL, D_MODEL, GLA_RANK, GLA_RANK,
                   QK, QK, D_MODEL, D_MODEL, N_HEADS, N_HEADS, N_HEADS, N_HEADS, D_MODEL, D_MODEL))
    seg = lambda i: w_in[:, o[i]:o[i + 1]]
    w_main = jnp.concatenate([seg(0), seg(1), seg(2), seg(3), seg(6), seg(7), seg(8), seg(9),
                              seg(14), seg(15)], axis=1).astype(BF16)
    w_lr = jnp.concatenate([seg(4), seg(5), jnp.zeros((D_MODEL, LANES - 2 * GLA_RANK), F32)], axis=1).astype(BF16)
    g4 = jnp.stack([seg(10), seg(11), seg(12), seg(13)], axis=-1)
    g4 = g4.reshape(D_MODEL, _GDN_GROUPS, 4 * _GDN_HEADS_PER_STEP)
    w_g = jnp.concatenate([g4, jnp.zeros((D_MODEL, _GDN_GROUPS, LANES - 4 * _GDN_HEADS_PER_STEP), F32)], axis=-1)
    w_g = w_g.reshape(D_MODEL, _GDN_GROUPS * LANES).astype(BF16)
    return w_main, w_lr, w_g


def _in_proj(x2d, ln_mix, w_main, w_lr, w_g, tm):
    n = x2d.shape[0]
    widths = _MAIN_WIDTHS
    const = lambda i: (0, 0)
    row = lambda i: (i, 0)
    out_shape = [jax.ShapeDtypeStruct((n, w), BF16) for w in widths]
    out_shape += [jax.ShapeDtypeStruct((n, LANES), F32), jax.ShapeDtypeStruct((n, _GDN_GROUPS * LANES), F32)]
    out_specs = [pl.BlockSpec((tm, w), row) for w in widths]
    out_specs += [pl.BlockSpec((tm, LANES), row), pl.BlockSpec((tm, _GDN_GROUPS * LANES), row)]
    single = pl.Buffered(1)
    return pl.pallas_call(
        _inproj_body,
        grid=(n // tm,),
        in_specs=[pl.BlockSpec((tm, D_MODEL), row),
                  pl.BlockSpec((1, D_MODEL), const),
                  pl.BlockSpec((D_MODEL, _MAIN_TOTAL), const, pipeline_mode=single),
                  pl.BlockSpec((D_MODEL, LANES), const, pipeline_mode=single),
                  pl.BlockSpec((D_MODEL, _GDN_GROUPS * LANES), const, pipeline_mode=single)],
        out_specs=out_specs,
        out_shape=out_shape,
        compiler_params=_cparams(("parallel",)),
        name="in_proj",
    )(x2d, ln_mix.reshape(1, D_MODEL), w_main, w_lr, w_g)


def _tri_masks(c):
    r = lax.broadcasted_iota(I32, (c, c), 0)
    s = lax.broadcasted_iota(I32, (c, c), 1)
    return r, s


def _gla_body(qf, kf, vf, lrf, qb, kb, vb, lrb, wf, bf, wb, bb, trif, trib, of, ob, sf, sb):
    @pl.when(pl.program_id(2) == 0)
    def _():
        sf[...] = jnp.zeros_like(sf)
        sb[...] = jnp.zeros_like(sb)

    tb = qf.shape[0]
    c = CHUNK
    nc = tb // c
    r, s = _tri_masks(c)
    streams = []
    for refs, reverse in (((qf, kf, vf, lrf, wf, bf, trif, of, sf), False),
                          ((qb, kb, vb, lrb, wb, bb, trib, ob, sb), True)):
        q_ref, k_ref, v_ref, lr_ref, w_ref, b_ref, tri_ref, o_ref, s_ref = refs
        logits = _dot(lr_ref[...].astype(BF16), w_ref[...]) + b_ref[...]
        gk = _log_sigmoid(logits) * (1.0 / GLA_NORMALIZER)
        cum = _chunk_cumsum(tri_ref[...], gk)
        streams.append(dict(q=q_ref, k=k_ref, v=v_ref, o=o_ref, s=s_ref, reverse=reverse, cum=cum,
                            incl=(r <= s) if reverse else (r >= s)))
    items = [(st, (nc - 1 - j) if st["reverse"] else j) for j in range(nc) for st in streams]
    work = []
    for st, ci in items:
        r0 = ci * c
        cum = st["cum"][r0:r0 + c]
        edge = r0 if st["reverse"] else r0 + c - 1
        tot = st["cum"][edge:edge + 1]
        q = st["q"][r0:r0 + c, :].astype(F32)
        k = st["k"][r0:r0 + c, :].astype(F32)
        work.append(dict(r0=r0, v=st["v"][r0:r0 + c, :],
                         q_dec=(q * jnp.exp(cum) * (DK ** -0.5)).astype(BF16),
                         k_inv=(k * jnp.exp(-cum)).astype(BF16),
                         k_end_t=jnp.transpose(k * jnp.exp(tot - cum)).astype(BF16),
                         dec=jnp.transpose(jnp.broadcast_to(jnp.exp(tot), (DK, DK)))))
    att = [jnp.where(st["incl"], _dot_nt(y["q_dec"], y["k_inv"]), 0.0).astype(BF16) for (st, _), y in zip(items, work)]
    av = [_dot(jnp.concatenate([a, y["k_end_t"]], axis=0), y["v"]) for a, y in zip(att, work)]
    for (st, ci), y, z in zip(items, work, av):
        r0 = y["r0"]
        s_ref = st["s"]
        state = s_ref[...]
        o = z[0:c] + _dot(y["q_dec"], state.astype(BF16))
        st["o"][r0:r0 + c, :] = o.astype(st["o"].dtype)
        s_ref[:, 0:DK] = state[:, 0:DK] * y["dec"] + z[c:c + DK, 0:DK]
        s_ref[:, DK:DV] = state[:, DK:DV] * y["dec"] + z[c:c + DK, DK:DV]


def _gla(aq, ak, av, lr, w_up, b_up, tb):
    bsz, l, _ = aq.shape
    nb = l // tb
    fwd = lambda b, h, i: (b, i, h)
    bwd = lambda b, h, i: (b, nb - 1 - i, h)
    fwd0 = lambda b, h, i: (b, i, 0)
    bwd0 = lambda b, h, i: (b, nb - 1 - i, 0)
    qk_spec = lambda m: pl.BlockSpec((None, tb, DK), m)
    v_spec = lambda m: pl.BlockSpec((None, tb, DV), m)
    lr_spec = lambda m: pl.BlockSpec((None, tb, LANES), m)
    in_specs = [qk_spec(fwd), qk_spec(fwd), v_spec(fwd), lr_spec(fwd0),
                qk_spec(bwd), qk_spec(bwd), v_spec(bwd), lr_spec(bwd0),
                pl.BlockSpec((LANES, DK), lambda b, h, i: (0, h)),
                pl.BlockSpec((1, DK), lambda b, h, i: (0, h)),
                pl.BlockSpec((LANES, DK), lambda b, h, i: (0, N_HEADS + h)),
                pl.BlockSpec((1, DK), lambda b, h, i: (0, N_HEADS + h)),
                _TRI_SPEC, _TRI_SPEC]
    out = jax.ShapeDtypeStruct((bsz, l, D_MODEL), BF16)
    return pl.pallas_call(
        _gla_body,
        grid=(bsz, N_HEADS, nb),
        in_specs=in_specs,
        out_specs=[v_spec(fwd), v_spec(bwd)],
        out_shape=[out, out],
        scratch_shapes=[pltpu.VMEM((DK, DV), F32), pltpu.VMEM((DK, DV), F32)],
        compiler_params=_cparams(("parallel", "parallel", "arbitrary")),
        name="gla",
    )(aq, ak, av, lr, aq, ak, av, lr, w_up, b_up, w_up, b_up, _chunk_tri(False), _chunk_tri(True))


def _pack_gla_up(w_up_f, b_up_f, w_up_b, b_up_b):
    w = jnp.zeros((LANES, 2 * QK), F32)
    w = w.at[0:GLA_RANK, 0:QK].set(w_up_f).at[GLA_RANK:2 * GLA_RANK, QK:2 * QK].set(w_up_b)
    b = jnp.concatenate([b_up_f, b_up_b]).reshape(1, 2 * QK).astype(F32)
    return w.astype(BF16), b


_HALO = 16


def _gdn_prep_body(x_ref, prev_ref, next_ref, w_ref, q_ref, k_ref, v_ref):
    i = pl.program_id(1)
    nb = pl.num_programs(1)
    tb = x_ref.shape[0]
    x = x_ref[...].astype(F32)
    prev_row = jnp.where(i > 0, prev_ref[_HALO - 1:_HALO, :].astype(F32), 0.0)
    next_row = jnp.where(i < nb - 1, next_ref[0:1, :].astype(F32), 0.0)
    ridx = lax.broadcasted_iota(I32, (tb, 1), 0)
    x_prev = jnp.where(ridx == 0, prev_row, pltpu.roll(x, 1, axis=0))
    x_next = jnp.where(ridx == tb - 1, next_row, pltpu.roll(x, tb - 1, axis=0))
    y = _silu(x_prev * w_ref[0:1, :] + x * w_ref[1:2, :] + x_next * w_ref[2:3, :])
    for h in range(N_HEADS):
        qh = y[:, h * DK:(h + 1) * DK]
        q_ref[:, h * DK:(h + 1) * DK] = (
            qh * (lax.rsqrt(jnp.sum(qh * qh, axis=-1, keepdims=True) + EPS) * DK ** -0.5)).astype(q_ref.dtype)
        kh = y[:, QK + h * DK:QK + (h + 1) * DK]
        k_ref[:, h * DK:(h + 1) * DK] = (
            kh * lax.rsqrt(jnp.sum(kh * kh, axis=-1, keepdims=True) + EPS)).astype(k_ref.dtype)
    v_ref[...] = y[:, 2 * QK:].astype(v_ref.dtype)


def _gdn_prep(bqkv, conv_w, tb):
    bsz, l, width = bqkv.shape
    nb = l // tb
    hb = tb // _HALO
    nh = l // _HALO
    return pl.pallas_call(
        _gdn_prep_body,
        grid=(bsz, nb),
        in_specs=[pl.BlockSpec((None, tb, width), lambda b, i: (b, i, 0)),
                  pl.BlockSpec((None, _HALO, width), lambda b, i: (b, jnp.maximum(i * hb - 1, 0), 0)),
                  pl.BlockSpec((None, _HALO, width), lambda b, i: (b, jnp.minimum((i + 1) * hb, nh - 1), 0)),
                  pl.BlockSpec((3, width), lambda b, i: (0, 0))],
        out_specs=[pl.BlockSpec((None, tb, QK), lambda b, i: (b, i, 0)),
                   pl.BlockSpec((None, tb, QK), lambda b, i: (b, i, 0)),
                   pl.BlockSpec((None, tb, D_MODEL), lambda b, i: (b, i, 0))],
        out_shape=[jax.ShapeDtypeStruct((bsz, l, QK), BF16), jax.ShapeDtypeStruct((bsz, l, QK), BF16),
                   jax.ShapeDtypeStruct((bsz, l, D_MODEL), BF16)],
        compiler_params=_cparams(("parallel", "parallel")),
        name="gdn_prep",
    )(bqkv, bqkv, bqkv, conv_w.astype(F32))


_G_BETA = 0
_G_A = 2


def _pack_gdn_params(a_log_f, dt_bias_f, a_log_b, dt_bias_b):
    p = jnp.zeros((8, N_HEADS, 4), F32)
    p = p.at[0, :, _G_A].set(-jnp.exp(a_log_f.astype(F32))).at[0, :, _G_A + 1].set(-jnp.exp(a_log_b.astype(F32)))
    p = p.at[1, :, _G_A].set(dt_bias_f.astype(F32)).at[1, :, _G_A + 1].set(dt_bias_b.astype(F32))
    p = p.reshape(8, _GDN_GROUPS, 4 * _GDN_HEADS_PER_STEP)
    p = jnp.concatenate([p, jnp.zeros((8, _GDN_GROUPS, LANES - 4 * _GDN_HEADS_PER_STEP), F32)], axis=-1)
    return p.reshape(8, _GDN_GROUPS * LANES)


def _each(fn, *lists):
    out = []
    for args in zip(*lists):
        out.append(fn(*args))
        yield
    return out


def _interleave(main, side, side_per_main):
    main_live = side_live = True
    while main_live or side_live:
        if main_live:
            main_live = next(main, "done") != "done"
        for _ in range(side_per_main if main_live else 1):
            if side_live:
                side_live = next(side, "done") != "done"


def _unit_tri_inverse_many(a_list, r, s, eye):
    c = CHUNK
    bf = lambda x: x.astype(BF16)
    same16 = (r >> 4) == (s >> 4)
    same32 = (r >> 5) == (s >> 5)
    off16 = jnp.logical_and(same32, jnp.logical_not(same16))
    d = [jnp.where(same16, a, 0.0) for a in a_list]
    n = [eye - x for x in d]
    p = yield from _each(lambda x: _dot(bf(x), bf(x)), d)
    for _ in range(2):
        res = yield from _each(lambda x, y: _dot(jnp.concatenate([bf(x), bf(y)], axis=0), bf(y)), n, p)
        n = [x + z[0:c] for x, z in zip(n, res)]
        p = [z[c:2 * c] for z in res]
    n = yield from _each(lambda x, y: x + _dot(bf(x), bf(y)), n, p)
    for e in ([jnp.where(off16, a, 0.0) for a in a_list], [jnp.where(same32, 0.0, a) for a in a_list]):
        t = yield from _each(lambda x, y: _dot(bf(x), bf(y)), n, e)
        n = yield from _each(lambda x, y: x - _dot(bf(y), bf(x)), n, t)
    return n


def _gdn_body(qf, kf, vf, gf, qb, kb, vb, gb, p_ref, trif, trib, of, ob, s_all, u_s, wq_s, att_s, ke_s, dec_s):
    step = pl.program_id(2)

    @pl.when(step == 0)
    def _():
        s_all[...] = jnp.zeros_like(s_all)
        for ref in (u_s, wq_s, att_s, ke_s, dec_s):
            ref[1] = jnp.zeros(ref.shape[1:], ref.dtype)

    tb = qf.shape[0]
    c = CHUNK
    nc = tb // c
    rd = (step + 1) % 2
    wr = step % 2
    order = [(hh, d, (nc - 1 - j) if d else j) for j in range(nc) for hh in range(_GDN_HEADS_PER_STEP) for d in (0, 1)]

    def recurrence():
        for idx, (hh, d, ci) in enumerate(order):
            r0 = ci * c
            o_ref = ob if d else of
            state = s_all[2 * hh + d]
            ws = _dot(wq_s[rd, idx], state.astype(BF16))
            v_new = (u_s[rd, idx] - ws[0:c]).astype(BF16)
            av = _dot(jnp.concatenate([att_s[rd, idx], ke_s[rd, idx]], axis=0), v_new)
            o = ws[c:2 * c] + av[0:c]
            o_ref[r0:r0 + c, hh * DV:(hh + 1) * DV] = o.astype(o_ref.dtype)
            s_all[2 * hh + d] = state * dec_s[rd, idx][0:1, 0:1] + av[c:c + DK]
            yield

    _interleave(recurrence(), _gdn_prepare(qf, kf, vf, gf, qb, kb, vb, gb, p_ref, trif, trib, order, wr,
                                           u_s, wq_s, att_s, ke_s, dec_s), side_per_main=_GDN_PREP_PER_REC)


_GDN_PREP_PER_REC = 11


def _gdn_prepare(qf, kf, vf, gf, qb, kb, vb, gb, p_ref, trif, trib, order, wr, u_s, wq_s, att_s, ke_s, dec_s):
    c = CHUNK
    r, s = _tri_masks(c)
    eye = jnp.where(r == s, 1.0, 0.0)
    incl = (r >= s, r <= s)
    strict = (r > s, r < s)
    streams = {}
    for refs, reverse in (((qf, kf, vf, gf, trif), False), ((qb, kb, vb, gb, trib), True)):
        q_ref, k_ref, v_ref, g_ref, tri_ref = refs
        gates = g_ref[...]
        g_all = p_ref[0:1, :] * _softplus(gates + p_ref[1:2, :])
        cum_all = _chunk_cumsum(tri_ref[...], g_all)
        shared = dict(q=q_ref, k=k_ref, v=v_ref, reverse=reverse,
                      beta=_sigmoid(gates), cum=cum_all, cum_t=jnp.transpose(cum_all))
        for hh in range(_GDN_HEADS_PER_STEP):
            streams[hh, int(reverse)] = dict(shared, hh=hh, la=4 * hh + _G_A + int(reverse),
                                             lb=4 * hh + _G_BETA + int(reverse))

    def chunk_inputs(item):
        hh, d, ci = item
        st = streams[hh, d]
        r0 = ci * c
        la, lb = st["la"], st["lb"]
        edge = r0 if st["reverse"] else r0 + c - 1
        qk_cols = slice(hh * DK, (hh + 1) * DK)
        ccol = st["cum"][r0:r0 + c, la:la + 1]
        crow = st["cum_t"][la:la + 1, r0:r0 + c]
        return dict(ccol=ccol, d=d, r0=r0, st=st,
                    tot=st["cum"][edge:edge + 1, la:la + 1],
                    beta=st["beta"][r0:r0 + c, lb:lb + 1],
                    decay=jnp.where(incl[d], jnp.exp(jnp.where(incl[d], ccol - crow, 0.0)), 0.0),
                    qb=st["q"][r0:r0 + c, qk_cols], kb=st["k"][r0:r0 + c, qk_cols])

    pre = yield from _each(chunk_inputs, order)
    kq = yield from _each(lambda x: _dot_nt(jnp.concatenate([x["kb"], x["qb"]], axis=0), x["kb"]), pre)
    a_list = [jnp.where(strict[x["d"]], y[0:c] * x["beta"] * x["decay"], 0.0) for x, y in zip(pre, kq)]
    t_inv = yield from _unit_tri_inverse_many(a_list, r, s, eye)

    def stage(idx, x, t, y):
        r0 = x["r0"]
        k = x["kb"].astype(F32)
        e_cum = jnp.exp(x["ccol"])
        v_cols = slice(x["st"]["hh"] * DV, (x["st"]["hh"] + 1) * DV)
        rhs = jnp.concatenate([(x["st"]["v"][r0:r0 + c, v_cols].astype(F32) * x["beta"]).astype(BF16),
                               (k * (x["beta"] * e_cum)).astype(BF16)], axis=1)
        uw = _dot(t.astype(BF16), rhs)
        u_s[wr, idx] = uw[:, 0:DV]
        wq_s[wr, idx] = jnp.concatenate([uw[:, DV:DV + DK].astype(BF16),
                                         (x["qb"].astype(F32) * e_cum).astype(BF16)], axis=0)
        att_s[wr, idx] = (y[c:2 * c] * x["decay"]).astype(BF16)
        ke_s[wr, idx] = jnp.transpose(k * jnp.exp(x["tot"] - x["ccol"])).astype(BF16)
        dec_s[wr, idx] = jnp.broadcast_to(jnp.exp(x["tot"]), dec_s.shape[2:])

    yield from _each(stage, range(len(pre)), pre, t_inv, kq)


def _chunk_tri(reverse):
    i = np.arange(CHUNK)
    tri = (i[:, None] <= i[None, :]) if reverse else (i[:, None] >= i[None, :])
    return jnp.asarray(np.tile(np.where(tri, 1.0, 0.0), (1, 3)), BF16)


_TRI_SPEC = pl.BlockSpec((CHUNK, 3 * CHUNK), lambda b, h, i: (0, 0))


def _gdn(cq, ck, cv, gates, gparams, tb):
    bsz, l, _ = cq.shape
    nb = l // tb
    hps = _GDN_HEADS_PER_STEP
    n_items = 2 * hps * (tb // CHUNK)
    fwd = lambda b, h, i: (b, jnp.minimum(i, nb - 1), h)
    bwd = lambda b, h, i: (b, nb - 1 - jnp.minimum(i, nb - 1), h)
    fwd_o = lambda b, h, i: (b, jnp.maximum(i - 1, 0), h)
    bwd_o = lambda b, h, i: (b, nb - 1 - jnp.maximum(i - 1, 0), h)
    qk_spec = lambda m: pl.BlockSpec((None, tb, hps * DK), m)
    v_spec = lambda m: pl.BlockSpec((None, tb, hps * DV), m)
    g_spec = lambda m: pl.BlockSpec((None, tb, LANES), m)
    const = lambda b, h, i: (0, 0)
    in_specs = [qk_spec(fwd), qk_spec(fwd), v_spec(fwd), g_spec(fwd),
                qk_spec(bwd), qk_spec(bwd), v_spec(bwd), g_spec(bwd),
                pl.BlockSpec((8, LANES), lambda b, h, i: (0, h)),
                _TRI_SPEC, _TRI_SPEC]
    out = jax.ShapeDtypeStruct((bsz, l, D_MODEL), BF16)
    return pl.pallas_call(
        _gdn_body,
        grid=(bsz, N_HEADS // hps, nb + 1),
        in_specs=in_specs,
        out_specs=[v_spec(fwd_o), v_spec(bwd_o)],
        out_shape=[out, out],
        scratch_shapes=[pltpu.VMEM((2 * hps, DK, DV), F32),
                        pltpu.VMEM((2, n_items, CHUNK, DV), F32),
                        pltpu.VMEM((2, n_items, 2 * CHUNK, DK), BF16),
                        pltpu.VMEM((2, n_items, CHUNK, CHUNK), BF16),
                        pltpu.VMEM((2, n_items, DK, CHUNK), BF16),
                        pltpu.VMEM((2, n_items, 8, LANES), F32)],
        compiler_params=_cparams(("parallel", "parallel", "arbitrary")),
        name="gdn",
    )(cq, ck, cv, gates, cq, ck, cv, gates, gparams, _chunk_tri(False), _chunk_tri(True))


def _rmsnorm(x, w):
    return x * lax.rsqrt(jnp.mean(x * x, axis=-1, keepdims=True) + EPS) * w


def _mem_kv_body(m_ref, g_ref, wk_ref, wv_ref, k_ref, v_ref):
    mn = _rmsnorm(m_ref[...], g_ref[...]).astype(BF16)
    k_ref[...] = _dot(mn, wk_ref[...]).astype(k_ref.dtype)
    v_ref[...] = _dot(mn, wv_ref[...]).astype(v_ref.dtype)


def _mem_kv(mem, ln_mem, wk, wv):
    bsz, m, _ = mem.shape
    const = lambda b: (0, 0)
    blk = pl.BlockSpec((None, m, D_MODEL), lambda b: (b, 0, 0))
    out = jax.ShapeDtypeStruct((bsz, m, D_MODEL), BF16)
    return pl.pallas_call(
        _mem_kv_body,
        grid=(bsz,),
        in_specs=[blk, pl.BlockSpec((1, D_MODEL), const),
                  pl.BlockSpec((D_MODEL, D_MODEL), const), pl.BlockSpec((D_MODEL, D_MODEL), const)],
        out_specs=[blk, blk],
        out_shape=[out, out],
        compiler_params=_cparams(("parallel",)),
        name="mem_kv",
    )(mem, ln_mem.reshape(1, D_MODEL), wk, wv)


_R_E1, _R_E2, _R_W1, _R_W2 = 0, 1, 2, 3


def _head_rmsnorm(o, w):
    parts = []
    for h in range(N_HEADS):
        oh = o[:, h * DV:(h + 1) * DV]
        parts.append(oh * lax.rsqrt(jnp.mean(oh * oh, axis=-1, keepdims=True) + EPS))
    return jnp.concatenate(parts, axis=-1) * w


def _route_tile(lg):
    neg = -1e30
    big = 1e9
    lane = lax.broadcasted_iota(I32, lg.shape, 1).astype(F32)
    gmask = lane < N_GROUPS
    gl = jnp.where(gmask, lg, neg)
    gmax = jnp.max(gl, axis=-1, keepdims=True)
    gidx = jnp.min(jnp.where(gl == gmax, lane, big), axis=-1, keepdims=True)
    gsum = jnp.sum(jnp.where(gmask, jnp.exp(gl - gmax), 0.0), axis=-1, keepdims=True)
    lo = N_GROUPS + EXP_PER_GROUP * gidx
    off = lane - lo
    el = jnp.where(jnp.abs(off - (EXP_PER_GROUP - 1) * 0.5) < EXP_PER_GROUP * 0.5, lg, neg)
    m1 = jnp.max(el, axis=-1, keepdims=True)
    i1 = jnp.min(jnp.where(el == m1, lane, big), axis=-1, keepdims=True)
    el2 = jnp.where(lane == i1, neg, el)
    m2 = jnp.max(el2, axis=-1, keepdims=True)
    i2 = jnp.min(jnp.where(el2 == m2, lane, big), axis=-1, keepdims=True)
    r = jnp.exp(m2 - m1)
    p1 = 1.0 / (1.0 + r)
    gw = 1.0 / gsum
    w1 = gw * p1
    w2 = gw * (r * p1)
    e1 = i1 - N_GROUPS
    e2 = i2 - N_GROUPS
    return jnp.where(lane == _R_E1, e1, jnp.where(lane == _R_E2, e2,
                     jnp.where(lane == _R_W1, w1, jnp.where(lane == _R_W2, w2, 0.0))))


def _post_body(x_ref, oaf, oab, obf, obb, ar, bg, ma, mb, na_ref, nb_ref, wout, lnxa, wq, kmem, vmem, wo, lnmoe,
               wrh, wrl, br, h2_ref, hn2_ref, route_ref):
    f = lambda ref: ref[...].astype(F32)
    oa = _head_rmsnorm(f(oaf) + f(oab), na_ref[...]) * _silu(f(ar))
    ob = _head_rmsnorm(f(obf) + f(obb), nb_ref[...]) * _silu(f(bg))
    y = _sigmoid(f(ma)) * oa + _sigmoid(f(mb)) * ob
    h1 = x_ref[...] + _dot(y.astype(BF16), wout[...])
    hn = _rmsnorm(h1, lnxa[...]).astype(BF16)
    q = _dot(hn, wq[...])
    outs = []
    for h in range(N_HEADS):
        sl = slice(h * XA_DH, (h + 1) * XA_DH)
        s = _dot_nt(q[:, sl].astype(BF16), kmem[:, sl]) * (XA_DH ** -0.5)
        p = jnp.exp(s - jnp.max(s, axis=-1, keepdims=True))
        denom = jnp.sum(p, axis=-1, keepdims=True)
        outs.append(_dot(p.astype(BF16), vmem[:, sl]) * (1.0 / denom))
    o = jnp.concatenate(outs, axis=-1)
    h2 = h1 + _dot(o.astype(BF16), wo[...])
    h2_ref[...] = h2
    hn2 = _rmsnorm(h2, lnmoe[...])
    hn2_ref[...] = hn2
    x_hi = hn2.astype(BF16)
    x_lo = (hn2 - x_hi.astype(F32)).astype(BF16)
    lg = _dot(x_hi, wrh[...]) + _dot(x_hi, wrl[...]) + _dot(x_lo, wrh[...]) + br[...]
    route_ref[...] = _route_tile(lg)


def _post(x, oaf, oab, obf, obb, ar, bg, ma, mb, kmem, vmem, p, tm):
    bsz, l, _ = x.shape
    nb = l // tm
    blk = pl.BlockSpec((None, tm, D_MODEL), lambda b, i: (b, i, 0))
    const = lambda b, i: (0, 0)
    vec = pl.BlockSpec((1, D_MODEL), const)
    mat = pl.BlockSpec((D_MODEL, D_MODEL), const, pipeline_mode=pl.Buffered(1))
    memb = pl.BlockSpec((None, kmem.shape[1], D_MODEL), lambda b, i: (b, 0, 0))
    rmat = pl.BlockSpec((D_MODEL, LANES), const)
    return pl.pallas_call(
        _post_body,
        grid=(bsz, nb),
        in_specs=[blk] * 9 + [vec, vec, mat, vec, mat, memb, memb, mat, vec, rmat, rmat,
                               pl.BlockSpec((1, LANES), const)],
        out_specs=[blk, blk, pl.BlockSpec((None, tm, LANES), lambda b, i: (b, i, 0))],
        out_shape=[jax.ShapeDtypeStruct((bsz, l, D_MODEL), F32), jax.ShapeDtypeStruct((bsz, l, D_MODEL), F32),
                   jax.ShapeDtypeStruct((bsz, l, LANES), F32)],
        compiler_params=_cparams(("parallel", "parallel")),
        name="post",
    )(x, oaf, oab, obf, obb, ar, bg, ma, mb, p["gla_onorm"], p["gdn_onorm"], p["w_out"], p["ln_xa"], p["xa_wq"],
      kmem, vmem, p["xa_wo"], p["ln_moe"], p["w_r_hi"], p["w_r_lo"], p["b_r"])


def _route_body(rt_ref, low_ref, rank_ref, cnt_ref, carry):
    @pl.when(pl.program_id(0) == 0)
    def _():
        carry[...] = jnp.zeros_like(carry)

    rt = rt_ref[...]
    lane = lax.broadcasted_iota(I32, rt.shape, 1).astype(F32)
    oh1 = jnp.where(lane == rt[:, _R_E1:_R_E1 + 1], 1.0, 0.0)
    oh2 = jnp.where(lane == rt[:, _R_E2:_R_E2 + 1], 1.0, 0.0)
    tot = oh1 + oh2
    before = _dot(low_ref[...], tot.astype(BF16)) + carry[0:1, :]
    r1 = jnp.sum(before * oh1, axis=-1, keepdims=True)
    r2 = jnp.sum(before * oh2, axis=-1, keepdims=True)
    rank_ref[...] = jnp.where(lane == 0, r1, jnp.where(lane == 1, r2, 0.0))
    new = carry[0:1, :] + jnp.sum(tot, axis=0, keepdims=True)
    carry[...] = jnp.broadcast_to(new, carry.shape)
    cnt_ref[...] = jnp.broadcast_to(new, cnt_ref.shape)


def _route(route2d, tr):
    n = route2d.shape[0]
    i = np.arange(tr)
    low = jnp.asarray(np.where(i[:, None] > i[None, :], 1.0, 0.0), BF16)
    return pl.pallas_call(
        _route_body,
        grid=(n // tr,),
        in_specs=[pl.BlockSpec((tr, LANES), lambda i: (i, 0)), pl.BlockSpec((tr, tr), lambda i: (0, 0))],
        out_specs=[pl.BlockSpec((tr, LANES), lambda i: (i, 0)), pl.BlockSpec((8, LANES), lambda i: (0, 0))],
        out_shape=[jax.ShapeDtypeStruct((n, LANES), F32), jax.ShapeDtypeStruct((8, LANES), F32)],
        scratch_shapes=[pltpu.VMEM((8, LANES), F32)],
        compiler_params=_cparams(("arbitrary",)),
        name="route",
    )(route2d, low)


_ROW_UNROLL = 8


def _scatter_body(dest_ref, hn_ref, xs_in, xs_out, sem):
    del xs_in
    ts = hn_ref.shape[0]

    def row_copy(t, d):
        return pltpu.make_async_copy(hn_ref.at[pl.ds(t, 1)], xs_out.at[pl.ds(d, 1)], sem)

    def issue(t, carry):
        row_copy(t, dest_ref[0, t]).start()
        row_copy(t, dest_ref[1, t]).start()
        return carry

    lax.fori_loop(0, ts, issue, 0, unroll=_ROW_UNROLL)

    def drain(t, carry):
        row_copy(0, 0).wait()
        row_copy(0, 0).wait()
        return carry

    lax.fori_loop(0, ts, drain, 0, unroll=_ROW_UNROLL)


def _scatter(dest, hn2d, xs_init, ts):
    n = hn2d.shape[0]
    return pl.pallas_call(
        _scatter_body,
        grid=(n // ts,),
        in_specs=[pl.BlockSpec((2, ts), lambda i: (0, i), memory_space=pltpu.SMEM),
                  pl.BlockSpec((ts, D_MODEL), lambda i: (i, 0)),
                  pl.BlockSpec(memory_space=pl.ANY)],
        out_specs=pl.BlockSpec(memory_space=pl.ANY),
        out_shape=jax.ShapeDtypeStruct(xs_init.shape, xs_init.dtype),
        scratch_shapes=[pltpu.SemaphoreType.DMA(())],
        input_output_aliases={2: 0},
        compiler_params=_cparams(("arbitrary",)),
        name="scatter",
    )(dest, hn2d, xs_init)


_FFN_BLOCK = 512


def _ffn_body(be_ref, nu_ref, xs_ref, wg_ref, wu_ref, wd_ref, ys_ref, wg_s, wu_s, wd_s):
    i = pl.program_id(0)
    fresh = jnp.logical_or(i == 0, be_ref[i] != be_ref[jnp.maximum(i - 1, 0)])

    @pl.when(fresh)
    def _():
        wg_s[...] = wg_ref[...].astype(BF16)
        wu_s[...] = wu_ref[...].astype(BF16)
        wd_s[...] = wd_ref[...].astype(BF16)

    @pl.when(i < nu_ref[0])
    def _():
        x = xs_ref[...].astype(BF16)
        hidden = _silu(_dot(x, wg_s[...])) * _dot(x, wu_s[...])
        ys_ref[...] = _dot(hidden.astype(BF16), wd_s[...])

    @pl.when(i >= nu_ref[0])
    def _():
        ys_ref[...] = jnp.zeros_like(ys_ref)


def _ffn(block_e, n_used, xs, w_gate, w_up, w_down):
    p_len = xs.shape[0]
    nblk = p_len // _FFN_BLOCK
    grid_spec = pltpu.PrefetchScalarGridSpec(
        num_scalar_prefetch=2,
        grid=(nblk,),
        in_specs=[pl.BlockSpec((_FFN_BLOCK, D_MODEL), lambda i, be, nu: (i, 0)),
                  pl.BlockSpec((None, D_MODEL, D_EXPERT), lambda i, be, nu: (be[i], 0, 0)),
                  pl.BlockSpec((None, D_MODEL, D_EXPERT), lambda i, be, nu: (be[i], 0, 0)),
                  pl.BlockSpec((None, D_EXPERT, D_MODEL), lambda i, be, nu: (be[i], 0, 0))],
        out_specs=pl.BlockSpec((_FFN_BLOCK, D_MODEL), lambda i, be, nu: (i, 0)),
        scratch_shapes=[pltpu.VMEM((D_MODEL, D_EXPERT), BF16), pltpu.VMEM((D_MODEL, D_EXPERT), BF16),
                        pltpu.VMEM((D_EXPERT, D_MODEL), BF16)])
    return pl.pallas_call(
        _ffn_body,
        grid_spec=grid_spec,
        out_shape=jax.ShapeDtypeStruct((p_len, D_MODEL), F32),
        compiler_params=_cparams(("arbitrary",)),
        name="ffn",
    )(block_e, n_used, xs, w_gate, w_up, w_down)


def _combine_body(dest_ref, h2_ref, rt_ref, ys_ref, g_ref, out_ref, buf, sem):
    ts = h2_ref.shape[0]

    def row_copy(k, t, d):
        return pltpu.make_async_copy(ys_ref.at[pl.ds(d, 1)], buf.at[k, pl.ds(t, 1)], sem)

    def issue(t, carry):
        row_copy(0, t, dest_ref[0, t]).start()
        row_copy(1, t, dest_ref[1, t]).start()
        return carry

    lax.fori_loop(0, ts, issue, 0, unroll=_ROW_UNROLL)

    def drain(t, carry):
        row_copy(0, 0, 0).wait()
        row_copy(1, 0, 0).wait()
        return carry

    lax.fori_loop(0, ts, drain, 0, unroll=_ROW_UNROLL)
    rt = rt_ref[...]
    w1 = rt[:, _R_W1:_R_W1 + 1]
    w2 = rt[:, _R_W2:_R_W2 + 1]
    h3 = h2_ref[...] + w1 * buf[0] + w2 * buf[1]
    out_ref[...] = _rmsnorm(h3, g_ref[...])


def _combine(dest, h2, route2d, ys, ln_final, ts):
    n = h2.shape[0]
    return pl.pallas_call(
        _combine_body,
        grid=(n // ts,),
        in_specs=[pl.BlockSpec((2, ts), lambda i: (0, i), memory_space=pltpu.SMEM),
                  pl.BlockSpec((ts, D_MODEL), lambda i: (i, 0)),
                  pl.BlockSpec((ts, LANES), lambda i: (i, 0)),
                  pl.BlockSpec(memory_space=pl.ANY),
                  pl.BlockSpec((1, D_MODEL), lambda i: (0, 0))],
        out_specs=pl.BlockSpec((ts, D_MODEL), lambda i: (i, 0)),
        out_shape=jax.ShapeDtypeStruct((n, D_MODEL), F32),
        scratch_shapes=[pltpu.VMEM((2, ts, D_MODEL), F32), pltpu.SemaphoreType.DMA(())],
        compiler_params=_cparams(("arbitrary",)),
        name="combine",
    )(dest, h2, route2d, ys, ln_final.reshape(1, D_MODEL))


def _moe(h2, hn2, route2d, p):
    n = h2.shape[0]
    rank, counts = _route(route2d, tr=min(512, n))
    counts = counts[0, :N_EXPERTS].astype(I32)
    blk = _FFN_BLOCK
    padded = (counts + blk - 1) // blk * blk
    pends = jnp.cumsum(padded)
    pstart = pends - padded
    nblk = 2 * n // blk + N_EXPERTS
    starts = jnp.arange(nblk, dtype=I32) * blk
    block_e = jnp.minimum(jnp.sum((pends[None, :] <= starts[:, None]).astype(I32), axis=1), N_EXPERTS - 1)
    n_used = (pends[-1:] // blk).astype(I32)
    eid = route2d[:, _R_E1:_R_E2 + 1].astype(I32)
    dest = (pstart[eid] + rank[:, 0:2].astype(I32)).T
    xs = _scatter(dest, hn2, jnp.zeros((nblk * blk, D_MODEL), F32), ts=256)
    ys = _ffn(block_e, n_used, xs, p["moe_w_gate"], p["moe_w_up"], p["moe_w_down"])
    return _combine(dest, h2, route2d, ys, p["ln_final"], ts=256)


def _encode_group(x, mem, p):
    bsz, l, _ = x.shape
    n = bsz * l
    r3 = lambda t: t.reshape(bsz, l, t.shape[-1])
    aq, ak, av, ar, bqkv, bg, ma, mb, lr, gates = _in_proj(
        x.reshape(n, D_MODEL), p["ln_mix"], p["w_main"], p["w_lr"], p["w_g"], tm=min(512, n))
    cq, ck, cv = _gdn_prep(r3(bqkv), p["gdn_conv"], tb=min(512, l))
    oaf, oab = _gla(r3(aq), r3(ak), r3(av), r3(lr), p["gla_w_up"], p["gla_b_up"], tb=min(512, l))
    obf, obb = _gdn(cq, ck, cv, r3(gates), p["gdn_params"], tb=min(512, l))
    kmem, vmem = _mem_kv(mem, p["ln_mem"], p["xa_wk"], p["xa_wv"])
    h2, hn2, route = _post(x, oaf, oab, obf, obb, r3(ar), r3(bg), r3(ma), r3(mb), kmem, vmem, p, tm=min(512, l))
    out = _moe(h2.reshape(n, D_MODEL), hn2.reshape(n, D_MODEL), route.reshape(n, LANES), p)
    return out.reshape(bsz, l, D_MODEL)


def kernel(x_prompt, x_sample, mem_prompt, mem_sample, ln_mix, w_in, gla_w_up_f, gla_b_up_f, gla_w_up_b, gla_b_up_b, gla_onorm, gdn_conv, gdn_a_log_f, gdn_dt_bias_f, gdn_a_log_b, gdn_dt_bias_b, gdn_onorm, w_out, ln_xa, ln_mem, xa_wq, xa_wk, xa_wv, xa_wo, ln_moe, moe_w_group, moe_b_group, moe_w_expert, moe_b_expert, moe_w_gate, moe_w_up, moe_w_down, ln_final):
    w_main, w_lr, w_g = _pack_in_weights(w_in[0])
    gla_w_up, gla_b_up = _pack_gla_up(gla_w_up_f[0], gla_b_up_f[0], gla_w_up_b[0], gla_b_up_b[0])
    vec = lambda t: t.reshape(1, D_MODEL).astype(F32)
    w_r = jnp.concatenate([moe_w_group[0], moe_w_expert[0],
                           jnp.zeros((D_MODEL, LANES - N_GROUPS - N_EXPERTS), F32)], axis=1)
    w_r_hi = w_r.astype(BF16)
    b_r = jnp.concatenate([moe_b_group[0], moe_b_expert[0], jnp.zeros((LANES - N_GROUPS - N_EXPERTS,), F32)])
    p = dict(ln_mix=ln_mix[0], w_main=w_main, w_lr=w_lr, w_g=w_g, gla_w_up=gla_w_up, gla_b_up=gla_b_up,
             gdn_conv=gdn_conv[0],
             gdn_params=_pack_gdn_params(gdn_a_log_f[0], gdn_dt_bias_f[0], gdn_a_log_b[0], gdn_dt_bias_b[0]),
             gla_onorm=vec(jnp.tile(gla_onorm[0], N_HEADS)), gdn_onorm=vec(jnp.tile(gdn_onorm[0], N_HEADS)),
             w_out=w_out[0].astype(BF16), ln_xa=vec(ln_xa[0]), ln_mem=ln_mem[0],
             xa_wq=xa_wq[0].astype(BF16), xa_wk=xa_wk[0].astype(BF16), xa_wv=xa_wv[0].astype(BF16),
             xa_wo=xa_wo[0].astype(BF16), ln_moe=vec(ln_moe[0]),
             w_r_hi=w_r_hi, w_r_lo=(w_r - w_r_hi.astype(F32)).astype(BF16), b_r=b_r.reshape(1, LANES),
             moe_w_gate=moe_w_gate[0], moe_w_up=moe_w_up[0], moe_w_down=moe_w_down[0], ln_final=ln_final)
    return (_encode_group(x_prompt, mem_prompt, p), _encode_group(x_sample, mem_sample, p))
```

```python
import functools
import math

import jax
import jax.numpy as jnp
import numpy as np
from jax import lax
from jax.experimental import pallas as pl
from jax.experimental.pallas import tpu as pltpu

F32 = jnp.float32
BF16 = jnp.bfloat16
I32 = jnp.int32

D_MODEL = 1024
EPS = 1e-6
N_HEADS = 4
DK = 128
DV = 256
QK = N_HEADS * DK
GLA_RANK = 16
GLA_NORMALIZER = 16.0
CHUNK = 64
XA_DH = D_MODEL // N_HEADS
N_GROUPS = 4
EXP_PER_GROUP = 8
N_EXPERTS = N_GROUPS * EXP_PER_GROUP
D_EXPERT = D_MODEL // 2
LANES = 128
_GDN_HEADS_PER_STEP = 2
_GDN_GROUPS = N_HEADS // _GDN_HEADS_PER_STEP

VMEM_LIMIT = 56 * 1024 * 1024


def _cparams(sem):
    return pltpu.CompilerParams(dimension_semantics=sem, vmem_limit_bytes=VMEM_LIMIT)


def _dot(a, b):
    return jnp.dot(a, b, preferred_element_type=F32)


def _dot_nt(a, b):
    return lax.dot_general(a, b, (((1,), (1,)), ((), ())), preferred_element_type=F32)


def _dot_tn(a, b):
    return lax.dot_general(a, b, (((0,), (0,)), ((), ())), preferred_element_type=F32)


def _split3(x):
    hi = x.astype(BF16)
    r = x - hi.astype(F32)
    mid = r.astype(BF16)
    lo = (r - mid.astype(F32)).astype(BF16)
    return hi, mid, lo


def _chunk_cumsum(tri3, x):
    hi, mid, lo = _split3(x)
    outs = []
    for r0 in range(0, x.shape[0], CHUNK):
        rows = slice(r0, r0 + CHUNK)
        outs.append(_dot(tri3, jnp.concatenate([hi[rows], mid[rows], lo[rows]], axis=0)))
    return jnp.concatenate(outs, axis=0)


def _dot3(a, b):
    a_hi = a.astype(BF16)
    a_lo = (a - a_hi.astype(F32)).astype(BF16)
    b_hi = b.astype(BF16)
    b_lo = (b - b_hi.astype(F32)).astype(BF16)
    return _dot(a_hi, b_hi) + _dot(a_hi, b_lo) + _dot(a_lo, b_hi)


def _sigmoid(x):
    return 0.5 * jnp.tanh(0.5 * x) + 0.5


def _silu(x):
    return x * _sigmoid(x)


def _softplus(x):
    return jnp.maximum(x, 0.0) + jnp.log(1.0 + jnp.exp(-jnp.abs(x)))


def _log_sigmoid(x):
    return jnp.minimum(x, 0.0) - jnp.log(1.0 + jnp.exp(-jnp.abs(x)))


_MAIN_WIDTHS = (QK, QK, D_MODEL, D_MODEL, 2 * QK + D_MODEL, D_MODEL, D_MODEL, D_MODEL)
_MAIN_TOTAL = sum(_MAIN_WIDTHS)
_COL_TILE = 512


def _inproj_body(x_ref, g_ref, w_ref, wl_ref, wg_ref, aq, ak, av, ar, bqkv, bg, ma, mb, lr, gates):
    x = x_ref[...]
    ms = jnp.mean(x * x, axis=-1, keepdims=True)
    xn = (x * lax.rsqrt(ms + EPS) * g_ref[...]).astype(BF16)
    col = 0
    for ref in (aq, ak, av, ar, bqkv, bg, ma, mb):
        width = ref.shape[-1]
        for c in range(0, width, _COL_TILE):
            ref[:, c:c + _COL_TILE] = _dot(xn, w_ref[:, col + c:col + c + _COL_TILE]).astype(ref.dtype)
        col += width
    lr[...] = _dot(xn, wl_ref[...])
    gates[...] = _dot(xn, wg_ref[...])


def _pack_in_weights(w_in):
    o = np.cumsum((0, QK, QK, D_MODEL, D_MODEL, GLA_RANK, GLA_RANK,
                   QK, QK, D_MODEL, D_MODEL, N_HEADS, N_HEADS, N_HEADS, N_HEADS, D_MODEL, D_MODEL))
    seg = lambda i: w_in[:, o[i]:o[i + 1]]
    w_main = jnp.concatenate([seg(0), seg(1), seg(2), seg(3), seg(6), seg(7), seg(8), seg(9),
                              seg(14), seg(15)], axis=1).astype(BF16)
    w_lr = jnp.concatenate([seg(4), seg(5), jnp.zeros((D_MODEL, LANES - 2 * GLA_RANK), F32)], axis=1).astype(BF16)
    g4 = jnp.stack([seg(10), seg(11), seg(12), seg(13)], axis=-1)
    g4 = g4.reshape(D_MODEL, _GDN_GROUPS, 4 * _GDN_HEADS_PER_STEP)
    w_g = jnp.concatenate([g4, jnp.zeros((D_MODEL, _GDN_GROUPS, LANES - 4 * _GDN_HEADS_PER_STEP), F32)], axis=-1)
    w_g = w_g.reshape(D_MODEL, _GDN_GROUPS * LANES).astype(BF16)
    return w_main, w_lr, w_g


def _in_proj(x2d, ln_mix, w_main, w_lr, w_g, tm):
    n = x2d.shape[0]
    widths = _MAIN_WIDTHS
    const = lambda i: (0, 0)
    row = lambda i: (i, 0)
    out_shape = [jax.ShapeDtypeStruct((n, w), BF16) for w in widths]
    out_shape += [jax.ShapeDtypeStruct((n, LANES), F32), jax.ShapeDtypeStruct((n, _GDN_GROUPS * LANES), F32)]
    out_specs = [pl.BlockSpec((tm, w), row) for w in widths]
    out_specs += [pl.BlockSpec((tm, LANES), row), pl.BlockSpec((tm, _GDN_GROUPS * LANES), row)]
    single = pl.Buffered(1)
    return pl.pallas_call(
        _inproj_body,
        grid=(n // tm,),
        in_specs=[pl.BlockSpec((tm, D_MODEL), row),
                  pl.BlockSpec((1, D_MODEL), const),
                  pl.BlockSpec((D_MODEL, _MAIN_TOTAL), const, pipeline_mode=single),
                  pl.BlockSpec((D_MODEL, LANES), const, pipeline_mode=single),
                  pl.BlockSpec((D_MODEL, _GDN_GROUPS * LANES), const, pipeline_mode=single)],
        out_specs=out_specs,
        out_shape=out_shape,
        compiler_params=_cparams(("parallel",)),
        name="in_proj",
    )(x2d, ln_mix.reshape(1, D_MODEL), w_main, w_lr, w_g)


def _tri_masks(c):
    r = lax.broadcasted_iota(I32, (c, c), 0)
    s = lax.broadcasted_iota(I32, (c, c), 1)
    return r, s


def _gla_body(qf, kf, vf, lrf, qb, kb, vb, lrb, wf, bf, wb, bb, trif, trib, of, ob, sf, sb):
    @pl.when(pl.program_id(2) == 0)
    def _():
        sf[...] = jnp.zeros_like(sf)
        sb[...] = jnp.zeros_like(sb)

    tb = qf.shape[0]
    c = CHUNK
    nc = tb // c
    r, s = _tri_masks(c)
    streams = []
    for refs, reverse in (((qf, kf, vf, lrf, wf, bf, trif, of, sf), False),
                          ((qb, kb, vb, lrb, wb, bb, trib, ob, sb), True)):
        q_ref, k_ref, v_ref, lr_ref, w_ref, b_ref, tri_ref, o_ref, s_ref = refs
        logits = _dot(lr_ref[...].astype(BF16), w_ref[...]) + b_ref[...]
        gk = _log_sigmoid(logits) * (1.0 / GLA_NORMALIZER)
        cum = _chunk_cumsum(tri_ref[...], gk)
        streams.append(dict(q=q_ref, k=k_ref, v=v_ref, o=o_ref, s=s_ref, reverse=reverse, cum=cum,
                            incl=(r <= s) if reverse else (r >= s)))
    items = [(st, (nc - 1 - j) if st["reverse"] else j) for j in range(nc) for st in streams]
    work = []
    for st, ci in items:
        r0 = ci * c
        cum = st["cum"][r0:r0 + c]
        edge = r0 if st["reverse"] else r0 + c - 1
        tot = st["cum"][edge:edge + 1]
        q = st["q"][r0:r0 + c, :].astype(F32)
        k = st["k"][r0:r0 + c, :].astype(F32)
        work.append(dict(r0=r0, v=st["v"][r0:r0 + c, :],
                         q_dec=(q * jnp.exp(cum) * (DK ** -0.5)).astype(BF16),
                         k_inv=(k * jnp.exp(-cum)).astype(BF16),
                         k_end_t=jnp.transpose(k * jnp.exp(tot - cum)).astype(BF16),
                         dec=jnp.transpose(jnp.broadcast_to(jnp.exp(tot), (DK, DK)))))
    att = [jnp.where(st["incl"], _dot_nt(y["q_dec"], y["k_inv"]), 0.0).astype(BF16) for (st, _), y in zip(items, work)]
    av = [_dot(jnp.concatenate([a, y["k_end_t"]], axis=0), y["v"]) for a, y in zip(att, work)]
    for (st, ci), y, z in zip(items, work, av):
        r0 = y["r0"]
        s_ref = st["s"]
        state = s_ref[...]
        o = z[0:c] + _dot(y["q_dec"], state.astype(BF16))
        st["o"][r0:r0 + c, :] = o.astype(st["o"].dtype)
        s_ref[:, 0:DK] = state[:, 0:DK] * y["dec"] + z[c:c + DK, 0:DK]
        s_ref[:, DK:DV] = state[:, DK:DV] * y["dec"] + z[c:c + DK, DK:DV]


def _gla(aq, ak, av, lr, w_up, b_up, tb):
    bsz, l, _ = aq.shape
    nb = l // tb
    fwd = lambda b, h, i: (b, i, h)
    bwd = lambda b, h, i: (b, nb - 1 - i, h)
    fwd0 = lambda b, h, i: (b, i, 0)
    bwd0 = lambda b, h, i: (b, nb - 1 - i, 0)
    qk_spec = lambda m: pl.BlockSpec((None, tb, DK), m)
    v_spec = lambda m: pl.BlockSpec((None, tb, DV), m)
    lr_spec = lambda m: pl.BlockSpec((None, tb, LANES), m)
    in_specs = [qk_spec(fwd), qk_spec(fwd), v_spec(fwd), lr_spec(fwd0),
                qk_spec(bwd), qk_spec(bwd), v_spec(bwd), lr_spec(bwd0),
                pl.BlockSpec((LANES, DK), lambda b, h, i: (0, h)),
                pl.BlockSpec((1, DK), lambda b, h, i: (0, h)),
                pl.BlockSpec((LANES, DK), lambda b, h, i: (0, N_HEADS + h)),
                pl.BlockSpec((1, DK), lambda b, h, i: (0, N_HEADS + h)),
                _TRI_SPEC, _TRI_SPEC]
    out = jax.ShapeDtypeStruct((bsz, l, D_MODEL), BF16)
    return pl.pallas_call(
        _gla_body,
        grid=(bsz, N_HEADS, nb),
        in_specs=in_specs,
        out_specs=[v_spec(fwd), v_spec(bwd)],
        out_shape=[out, out],
        scratch_shapes=[pltpu.VMEM((DK, DV), F32), pltpu.VMEM((DK, DV), F32)],
        compiler_params=_cparams(("parallel", "parallel", "arbitrary")),
        name="gla",
    )(aq, ak, av, lr, aq, ak, av, lr, w_up, b_up, w_up, b_up, _chunk_tri(False), _chunk_tri(True))


def _pack_gla_up(w_up_f, b_up_f, w_up_b, b_up_b):
    w = jnp.zeros((LANES, 2 * QK), F32)
    w = w.at[0:GLA_RANK, 0:QK].set(w_up_f).at[GLA_RANK:2 * GLA_RANK, QK:2 * QK].set(w_up_b)
    b = jnp.concatenate([b_up_f, b_up_b]).reshape(1, 2 * QK).astype(F32)
    return w.astype(BF16), b


_HALO = 16


def _gdn_prep_body(x_ref, prev_ref, next_ref, w_ref, q_ref, k_ref, v_ref):
    i = pl.program_id(1)
    nb = pl.num_programs(1)
    tb = x_ref.shape[0]
    x = x_ref[...].astype(F32)
    prev_row = jnp.where(i > 0, prev_ref[_HALO - 1:_HALO, :].astype(F32), 0.0)
    next_row = jnp.where(i < nb - 1, next_ref[0:1, :].astype(F32), 0.0)
    ridx = lax.broadcasted_iota(I32, (tb, 1), 0)
    x_prev = jnp.where(ridx == 0, prev_row, pltpu.roll(x, 1, axis=0))
    x_next = jnp.where(ridx == tb - 1, next_row, pltpu.roll(x, tb - 1, axis=0))
    y = _silu(x_prev * w_ref[0:1, :] + x * w_ref[1:2, :] + x_next * w_ref[2:3, :])
    for h in range(N_HEADS):
        qh = y[:, h * DK:(h + 1) * DK]
        q_ref[:, h * DK:(h + 1) * DK] = (
            qh * (lax.rsqrt(jnp.sum(qh * qh, axis=-1, keepdims=True) + EPS) * DK ** -0.5)).astype(q_ref.dtype)
        kh = y[:, QK + h * DK:QK + (h + 1) * DK]
        k_ref[:, h * DK:(h + 1) * DK] = (
            kh * lax.rsqrt(jnp.sum(kh * kh, axis=-1, keepdims=True) + EPS)).astype(k_ref.dtype)
    v_ref[...] = y[:, 2 * QK:].astype(v_ref.dtype)


def _gdn_prep(bqkv, conv_w, tb):
    bsz, l, width = bqkv.shape
    nb = l // tb
    hb = tb // _HALO
    nh = l // _HALO
    return pl.pallas_call(
        _gdn_prep_body,
        grid=(bsz, nb),
        in_specs=[pl.BlockSpec((None, tb, width), lambda b, i: (b, i, 0)),
                  pl.BlockSpec((None, _HALO, width), lambda b, i: (b, jnp.maximum(i * hb - 1, 0), 0)),
                  pl.BlockSpec((None, _HALO, width), lambda b, i: (b, jnp.minimum((i + 1) * hb, nh - 1), 0)),
                  pl.BlockSpec((3, width), lambda b, i: (0, 0))],
        out_specs=[pl.BlockSpec((None, tb, QK), lambda b, i: (b, i, 0)),
                   pl.BlockSpec((None, tb, QK), lambda b, i: (b, i, 0)),
                   pl.BlockSpec((None, tb, D_MODEL), lambda b, i: (b, i, 0))],
        out_shape=[jax.ShapeDtypeStruct((bsz, l, QK), BF16), jax.ShapeDtypeStruct((bsz, l, QK), BF16),
                   jax.ShapeDtypeStruct((bsz, l, D_MODEL), BF16)],
        compiler_params=_cparams(("parallel", "parallel")),
        name="gdn_prep",
    )(bqkv, bqkv, bqkv, conv_w.astype(F32))


_G_BETA = 0
_G_A = 2


def _pack_gdn_params(a_log_f, dt_bias_f, a_log_b, dt_bias_b):
    p = jnp.zeros((8, N_HEADS, 4), F32)
    p = p.at[0, :, _G_A].set(-jnp.exp(a_log_f.astype(F32))).at[0, :, _G_A + 1].set(-jnp.exp(a_log_b.astype(F32)))
    p = p.at[1, :, _G_A].set(dt_bias_f.astype(F32)).at[1, :, _G_A + 1].set(dt_bias_b.astype(F32))
    p = p.reshape(8, _GDN_GROUPS, 4 * _GDN_HEADS_PER_STEP)
    p = jnp.concatenate([p, jnp.zeros((8, _GDN_GROUPS, LANES - 4 * _GDN_HEADS_PER_STEP), F32)], axis=-1)
    return p.reshape(8, _GDN_GROUPS * LANES)


def _each(fn, *lists):
    out = []
    for args in zip(*lists):
        out.append(fn(*args))
        yield
    return out


def _interleave(main, side, side_per_main):
    main_live = side_live = True
    while main_live or side_live:
        if main_live:
            main_live = next(main, "done") != "done"
        for _ in range(side_per_main if main_live else 1):
            if side_live:
                side_live = next(side, "done") != "done"


def _unit_tri_inverse_many(a_list, r, s, eye):
    c = CHUNK
    bf = lambda x: x.astype(BF16)
    same16 = (r >> 4) == (s >> 4)
    same32 = (r >> 5) == (s >> 5)
    off16 = jnp.logical_and(same32, jnp.logical_not(same16))
    d = [jnp.where(same16, a, 0.0) for a in a_list]
    n = [eye - x for x in d]
    p = yield from _each(lambda x: _dot(bf(x), bf(x)), d)
    for _ in range(2):
        res = yield from _each(lambda x, y: _dot(jnp.concatenate([bf(x), bf(y)], axis=0), bf(y)), n, p)
        n = [x + z[0:c] for x, z in zip(n, res)]
        p = [z[c:2 * c] for z in res]
    n = yield from _each(lambda x, y: x + _dot(bf(x), bf(y)), n, p)
    for e in ([jnp.where(off16, a, 0.0) for a in a_list], [jnp.where(same32, 0.0, a) for a in a_list]):
        t = yield from _each(lambda x, y: _dot(bf(x), bf(y)), n, e)
        n = yield from _each(lambda x, y: x - _dot(bf(y), bf(x)), n, t)
    return n


def _gdn_body(qf, kf, vf, gf, qb, kb, vb, gb, p_ref, trif, trib, of, ob, s_all, u_s, wq_s, att_s, ke_s, dec_s, *, nb):
    step = pl.program_id(0)

    @pl.when(step == 0)
    def _():
        for ref in (u_s, wq_s, att_s, ke_s, dec_s):
            ref[1] = jnp.zeros(ref.shape[1:], ref.dtype)

    @pl.when(jnp.maximum(step - 1, 0) % nb == 0)
    def _():
        s_all[...] = jnp.zeros_like(s_all)

    tb = qf.shape[0]
    c = CHUNK
    nc = tb // c
    rd = (step + 1) % 2
    wr = step % 2
    order = [(hh, d, (nc - 1 - j) if d else j) for j in range(nc) for hh in range(_GDN_HEADS_PER_STEP) for d in (0, 1)]

    def recurrence():
        for idx, (hh, d, ci) in enumerate(order):
            r0 = ci * c
            o_ref = ob if d else of
            state = s_all[2 * hh + d]
            ws = _dot(wq_s[rd, idx], state.astype(BF16))
            v_new = (u_s[rd, idx] - ws[0:c]).astype(BF16)
            av = _dot(jnp.concatenate([att_s[rd, idx], ke_s[rd, idx]], axis=0), v_new)
            o = ws[c:2 * c] + av[0:c]
            o_ref[r0:r0 + c, hh * DV:(hh + 1) * DV] = o.astype(o_ref.dtype)
            s_all[2 * hh + d] = state * dec_s[rd, idx][0:1, 0:1] + av[c:c + DK]
            yield

    _interleave(recurrence(), _gdn_prepare(qf, kf, vf, gf, qb, kb, vb, gb, p_ref, trif, trib, order, wr,
                                           u_s, wq_s, att_s, ke_s, dec_s), side_per_main=_GDN_PREP_PER_REC)


_GDN_PREP_PER_REC = 11


def _gdn_prepare(qf, kf, vf, gf, qb, kb, vb, gb, p_ref, trif, trib, order, wr, u_s, wq_s, att_s, ke_s, dec_s):
    c = CHUNK
    r, s = _tri_masks(c)
    eye = jnp.where(r == s, 1.0, 0.0)
    incl = (r >= s, r <= s)
    strict = (r > s, r < s)
    streams = {}
    for refs, reverse in (((qf, kf, vf, gf, trif), False), ((qb, kb, vb, gb, trib), True)):
        q_ref, k_ref, v_ref, g_ref, tri_ref = refs
        gates = g_ref[...]
        g_all = p_ref[0:1, :] * _softplus(gates + p_ref[1:2, :])
        cum_all = _chunk_cumsum(tri_ref[...], g_all)
        shared = dict(q=q_ref, k=k_ref, v=v_ref, reverse=reverse,
                      beta=_sigmoid(gates), cum=cum_all, cum_t=jnp.transpose(cum_all))
        for hh in range(_GDN_HEADS_PER_STEP):
            streams[hh, int(reverse)] = dict(shared, hh=hh, la=4 * hh + _G_A + int(reverse),
                                             lb=4 * hh + _G_BETA + int(reverse))

    def chunk_inputs(item):
        hh, d, ci = item
        st = streams[hh, d]
        r0 = ci * c
        la, lb = st["la"], st["lb"]
        edge = r0 if st["reverse"] else r0 + c - 1
        qk_cols = slice(hh * DK, (hh + 1) * DK)
        ccol = st["cum"][r0:r0 + c, la:la + 1]
        crow = st["cum_t"][la:la + 1, r0:r0 + c]
        return dict(ccol=ccol, d=d, r0=r0, st=st,
                    tot=st["cum"][edge:edge + 1, la:la + 1],
                    beta=st["beta"][r0:r0 + c, lb:lb + 1],
                    decay=jnp.where(incl[d], jnp.exp(jnp.where(incl[d], ccol - crow, 0.0)), 0.0),
                    qb=st["q"][r0:r0 + c, qk_cols], kb=st["k"][r0:r0 + c, qk_cols])

    pre = yield from _each(chunk_inputs, order)
    kq = yield from _each(lambda x: _dot_nt(jnp.concatenate([x["kb"], x["qb"]], axis=0), x["kb"]), pre)
    a_list = [jnp.where(strict[x["d"]], y[0:c] * x["beta"] * x["decay"], 0.0) for x, y in zip(pre, kq)]
    t_inv = yield from _unit_tri_inverse_many(a_list, r, s, eye)

    def stage(idx, x, t, y):
        r0 = x["r0"]
        k = x["kb"].astype(F32)
        e_cum = jnp.exp(x["ccol"])
        v_cols = slice(x["st"]["hh"] * DV, (x["st"]["hh"] + 1) * DV)
        rhs = jnp.concatenate([(x["st"]["v"][r0:r0 + c, v_cols].astype(F32) * x["beta"]).astype(BF16),
                               (k * (x["beta"] * e_cum)).astype(BF16)], axis=1)
        uw = _dot(t.astype(BF16), rhs)
        u_s[wr, idx] = uw[:, 0:DV]
        wq_s[wr, idx] = jnp.concatenate([uw[:, DV:DV + DK].astype(BF16),
                                         (x["qb"].astype(F32) * e_cum).astype(BF16)], axis=0)
        att_s[wr, idx] = (y[c:2 * c] * x["decay"]).astype(BF16)
        ke_s[wr, idx] = jnp.transpose(k * jnp.exp(x["tot"] - x["ccol"])).astype(BF16)
        dec_s[wr, idx] = jnp.broadcast_to(jnp.exp(x["tot"]), dec_s.shape[2:])

    yield from _each(stage, range(len(pre)), pre, t_inv, kq)


def _chunk_tri(reverse):
    i = np.arange(CHUNK)
    tri = (i[:, None] <= i[None, :]) if reverse else (i[:, None] >= i[None, :])
    return jnp.asarray(np.tile(np.where(tri, 1.0, 0.0), (1, 3)), BF16)


_TRI_SPEC = pl.BlockSpec((CHUNK, 3 * CHUNK), lambda b, h, i: (0, 0))


def _gdn(cq, ck, cv, gates, gparams, tb):
    bsz, l, _ = cq.shape
    nb = l // tb
    hps = _GDN_HEADS_PER_STEP
    n_items = 2 * hps * (tb // CHUNK)
    total = bsz * _GDN_GROUPS * nb

    def position(s):
        seq = s // nb
        return seq // _GDN_GROUPS, s % nb, seq % _GDN_GROUPS

    def fwd(s):
        b, i, g = position(jnp.minimum(s, total - 1))
        return b, i, g

    def bwd(s):
        b, i, g = position(jnp.minimum(s, total - 1))
        return b, nb - 1 - i, g

    def fwd_o(s):
        b, i, g = position(jnp.maximum(s - 1, 0))
        return b, i, g

    def bwd_o(s):
        b, i, g = position(jnp.maximum(s - 1, 0))
        return b, nb - 1 - i, g

    qk_spec = lambda m: pl.BlockSpec((None, tb, hps * DK), m)
    v_spec = lambda m: pl.BlockSpec((None, tb, hps * DV), m)
    g_spec = lambda m: pl.BlockSpec((None, tb, LANES), m)
    tri_spec = pl.BlockSpec((CHUNK, 3 * CHUNK), lambda s: (0, 0))
    in_specs = [qk_spec(fwd), qk_spec(fwd), v_spec(fwd), g_spec(fwd),
                qk_spec(bwd), qk_spec(bwd), v_spec(bwd), g_spec(bwd),
                pl.BlockSpec((8, LANES), lambda s: (0, position(jnp.minimum(s, total - 1))[2])),
                tri_spec, tri_spec]
    out = jax.ShapeDtypeStruct((bsz, l, D_MODEL), BF16)
    return pl.pallas_call(
        functools.partial(_gdn_body, nb=nb),
        grid=(total + 1,),
        in_specs=in_specs,
        out_specs=[v_spec(fwd_o), v_spec(bwd_o)],
        out_shape=[out, out],
        scratch_shapes=[pltpu.VMEM((2 * hps, DK, DV), F32),
                        pltpu.VMEM((2, n_items, CHUNK, DV), F32),
                        pltpu.VMEM((2, n_items, 2 * CHUNK, DK), BF16),
                        pltpu.VMEM((2, n_items, CHUNK, CHUNK), BF16),
                        pltpu.VMEM((2, n_items, DK, CHUNK), BF16),
                        pltpu.VMEM((2, n_items, 8, LANES), F32)],
        compiler_params=_cparams(("arbitrary",)),
        name="gdn",
    )(cq, ck, cv, gates, cq, ck, cv, gates, gparams, _chunk_tri(False), _chunk_tri(True))


def _rmsnorm(x, w):
    return x * lax.rsqrt(jnp.mean(x * x, axis=-1, keepdims=True) + EPS) * w


def _mem_kv_body(m_ref, g_ref, wk_ref, wv_ref, k_ref, v_ref):
    mn = _rmsnorm(m_ref[...], g_ref[...]).astype(BF16)
    k_ref[...] = _dot(mn, wk_ref[...]).astype(k_ref.dtype)
    v_ref[...] = _dot(mn, wv_ref[...]).astype(v_ref.dtype)


def _mem_kv(mem, ln_mem, wk, wv):
    bsz, m, _ = mem.shape
    const = lambda b: (0, 0)
    blk = pl.BlockSpec((None, m, D_MODEL), lambda b: (b, 0, 0))
    out = jax.ShapeDtypeStruct((bsz, m, D_MODEL), BF16)
    return pl.pallas_call(
        _mem_kv_body,
        grid=(bsz,),
        in_specs=[blk, pl.BlockSpec((1, D_MODEL), const),
                  pl.BlockSpec((D_MODEL, D_MODEL), const), pl.BlockSpec((D_MODEL, D_MODEL), const)],
        out_specs=[blk, blk],
        out_shape=[out, out],
        compiler_params=_cparams(("parallel",)),
        name="mem_kv",
    )(mem, ln_mem.reshape(1, D_MODEL), wk, wv)


_R_E1, _R_E2, _R_W1, _R_W2 = 0, 1, 2, 3


def _head_rmsnorm(o, w):
    parts = []
    for h in range(N_HEADS):
        oh = o[:, h * DV:(h + 1) * DV]
        parts.append(oh * lax.rsqrt(jnp.mean(oh * oh, axis=-1, keepdims=True) + EPS))
    return jnp.concatenate(parts, axis=-1) * w


def _route_tile(lg):
    neg = -1e30
    big = 1e9
    lane = lax.broadcasted_iota(I32, lg.shape, 1).astype(F32)
    gmask = lane < N_GROUPS
    gl = jnp.where(gmask, lg, neg)
    gmax = jnp.max(gl, axis=-1, keepdims=True)
    gidx = jnp.min(jnp.where(gl == gmax, lane, big), axis=-1, keepdims=True)
    gsum = jnp.sum(jnp.where(gmask, jnp.exp(gl - gmax), 0.0), axis=-1, keepdims=True)
    lo = N_GROUPS + EXP_PER_GROUP * gidx
    off = lane - lo
    el = jnp.where(jnp.abs(off - (EXP_PER_GROUP - 1) * 0.5) < EXP_PER_GROUP * 0.5, lg, neg)
    m1 = jnp.max(el, axis=-1, keepdims=True)
    i1 = jnp.min(jnp.where(el == m1, lane, big), axis=-1, keepdims=True)
    el2 = jnp.where(lane == i1, neg, el)
    m2 = jnp.max(el2, axis=-1, keepdims=True)
    i2 = jnp.min(jnp.where(el2 == m2, lane, big), axis=-1, keepdims=True)
    r = jnp.exp(m2 - m1)
    p1 = 1.0 / (1.0 + r)
    gw = 1.0 / gsum
    w1 = gw * p1
    w2 = gw * (r * p1)
    e1 = i1 - N_GROUPS
    e2 = i2 - N_GROUPS
    return jnp.where(lane == _R_E1, e1, jnp.where(lane == _R_E2, e2,
                     jnp.where(lane == _R_W1, w1, jnp.where(lane == _R_W2, w2, 0.0))))


def _post_body(x_ref, oaf, oab, obf, obb, ar, bg, ma, mb, na_ref, nb_ref, wout, lnxa, wq, kmem, vmem, wo, lnmoe,
               wrh, wrl, br, h2_ref, hn2_ref, route_ref):
    f = lambda ref: ref[...].astype(F32)
    oa = _head_rmsnorm(f(oaf) + f(oab), na_ref[...]) * _silu(f(ar))
    ob = _head_rmsnorm(f(obf) + f(obb), nb_ref[...]) * _silu(f(bg))
    y = _sigmoid(f(ma)) * oa + _sigmoid(f(mb)) * ob
    h1 = x_ref[...] + _dot(y.astype(BF16), wout[...])
    hn = _rmsnorm(h1, lnxa[...]).astype(BF16)
    q = _dot(hn, wq[...])
    outs = []
    for h in range(N_HEADS):
        sl = slice(h * XA_DH, (h + 1) * XA_DH)
        s = _dot_nt(q[:, sl].astype(BF16), kmem[:, sl]) * (XA_DH ** -0.5)
        p = jnp.exp(s - jnp.max(s, axis=-1, keepdims=True))
        denom = jnp.sum(p, axis=-1, keepdims=True)
        outs.append(_dot(p.astype(BF16), vmem[:, sl]) * (1.0 / denom))
    o = jnp.concatenate(outs, axis=-1)
    h2 = h1 + _dot(o.astype(BF16), wo[...])
    h2_ref[...] = h2
    hn2 = _rmsnorm(h2, lnmoe[...])
    hn2_ref[...] = hn2
    x_hi = hn2.astype(BF16)
    x_lo = (hn2 - x_hi.astype(F32)).astype(BF16)
    lg = _dot(x_hi, wrh[...]) + _dot(x_hi, wrl[...]) + _dot(x_lo, wrh[...]) + br[...]
    route_ref[...] = _route_tile(lg)


def _post(x, oaf, oab, obf, obb, ar, bg, ma, mb, kmem, vmem, p, tm):
    bsz, l, _ = x.shape
    nb = l // tm
    blk = pl.BlockSpec((None, tm, D_MODEL), lambda b, i: (b, i, 0))
    const = lambda b, i: (0, 0)
    vec = pl.BlockSpec((1, D_MODEL), const)
    mat = pl.BlockSpec((D_MODEL, D_MODEL), const, pipeline_mode=pl.Buffered(1))
    memb = pl.BlockSpec((None, kmem.shape[1], D_MODEL), lambda b, i: (b, 0, 0))
    rmat = pl.BlockSpec((D_MODEL, LANES), const)
    return pl.pallas_call(
        _post_body,
        grid=(bsz, nb),
        in_specs=[blk] * 9 + [vec, vec, mat, vec, mat, memb, memb, mat, vec, rmat, rmat,
                               pl.BlockSpec((1, LANES), const)],
        out_specs=[blk, blk, pl.BlockSpec((None, tm, LANES), lambda b, i: (b, i, 0))],
        out_shape=[jax.ShapeDtypeStruct((bsz, l, D_MODEL), F32), jax.ShapeDtypeStruct((bsz, l, D_MODEL), F32),
                   jax.ShapeDtypeStruct((bsz, l, LANES), F32)],
        compiler_params=_cparams(("parallel", "parallel")),
        name="post",
    )(x, oaf, oab, obf, obb, ar, bg, ma, mb, p["gla_onorm"], p["gdn_onorm"], p["w_out"], p["ln_xa"], p["xa_wq"],
      kmem, vmem, p["xa_wo"], p["ln_moe"], p["w_r_hi"], p["w_r_lo"], p["b_r"])


def _route_body(rt_ref, low_ref, rank_ref, cnt_ref, carry):
    @pl.when(pl.program_id(0) == 0)
    def _():
        carry[...] = jnp.zeros_like(carry)

    rt = rt_ref[...]
    lane = lax.broadcasted_iota(I32, rt.shape, 1).astype(F32)
    oh1 = jnp.where(lane == rt[:, _R_E1:_R_E1 + 1], 1.0, 0.0)
    oh2 = jnp.where(lane == rt[:, _R_E2:_R_E2 + 1], 1.0, 0.0)
    tot = oh1 + oh2
    before = _dot(low_ref[...], tot.astype(BF16)) + carry[0:1, :]
    r1 = jnp.sum(before * oh1, axis=-1, keepdims=True)
    r2 = jnp.sum(before * oh2, axis=-1, keepdims=True)
    rank_ref[...] = jnp.where(lane == 0, r1, jnp.where(lane == 1, r2, 0.0))
    new = carry[0:1, :] + jnp.sum(tot, axis=0, keepdims=True)
    carry[...] = jnp.broadcast_to(new, carry.shape)
    cnt_ref[...] = jnp.broadcast_to(new, cnt_ref.shape)


def _route(route2d, tr):
    n = route2d.shape[0]
    i = np.arange(tr)
    low = jnp.asarray(np.where(i[:, None] > i[None, :], 1.0, 0.0), BF16)
    return pl.pallas_call(
        _route_body,
        grid=(n // tr,),
        in_specs=[pl.BlockSpec((tr, LANES), lambda i: (i, 0)), pl.BlockSpec((tr, tr), lambda i: (0, 0))],
        out_specs=[pl.BlockSpec((tr, LANES), lambda i: (i, 0)), pl.BlockSpec((8, LANES), lambda i: (0, 0))],
        out_shape=[jax.ShapeDtypeStruct((n, LANES), F32), jax.ShapeDtypeStruct((8, LANES), F32)],
        scratch_shapes=[pltpu.VMEM((8, LANES), F32)],
        compiler_params=_cparams(("arbitrary",)),
        name="route",
    )(route2d, low)


_ROW_UNROLL = 8


def _scatter_body(dest_ref, hn_ref, xs_in, xs_out, sem):
    del xs_in
    ts = hn_ref.shape[0]

    def row_copy(t, d):
        return pltpu.make_async_copy(hn_ref.at[pl.ds(t, 1)], xs_out.at[pl.ds(d, 1)], sem)

    def issue(t, carry):
        row_copy(t, dest_ref[0, t]).start()
        row_copy(t, dest_ref[1, t]).start()
        return carry

    lax.fori_loop(0, ts, issue, 0, unroll=_ROW_UNROLL)

    def drain(t, carry):
        row_copy(0, 0).wait()
        row_copy(0, 0).wait()
        return carry

    lax.fori_loop(0, ts, drain, 0, unroll=_ROW_UNROLL)


def _scatter(dest, hn2d, xs_init, ts):
    n = hn2d.shape[0]
    return pl.pallas_call(
        _scatter_body,
        grid=(n // ts,),
        in_specs=[pl.BlockSpec((2, ts), lambda i: (0, i), memory_space=pltpu.SMEM),
                  pl.BlockSpec((ts, D_MODEL), lambda i: (i, 0)),
                  pl.BlockSpec(memory_space=pl.ANY)],
        out_specs=pl.BlockSpec(memory_space=pl.ANY),
        out_shape=jax.ShapeDtypeStruct(xs_init.shape, xs_init.dtype),
        scratch_shapes=[pltpu.SemaphoreType.DMA(())],
        input_output_aliases={2: 0},
        compiler_params=_cparams(("arbitrary",)),
        name="scatter",
    )(dest, hn2d, xs_init)


_FFN_BLOCK = 512


def _ffn_body(be_ref, nu_ref, xs_ref, wg_ref, wu_ref, wd_ref, ys_ref, wg_s, wu_s, wd_s):
    i = pl.program_id(0)
    fresh = jnp.logical_or(i == 0, be_ref[i] != be_ref[jnp.maximum(i - 1, 0)])

    @pl.when(fresh)
    def _():
        wg_s[...] = wg_ref[...].astype(BF16)
        wu_s[...] = wu_ref[...].astype(BF16)
        wd_s[...] = wd_ref[...].astype(BF16)

    @pl.when(i < nu_ref[0])
    def _():
        x = xs_ref[...].astype(BF16)
        hidden = _silu(_dot(x, wg_s[...])) * _dot(x, wu_s[...])
        ys_ref[...] = _dot(hidden.astype(BF16), wd_s[...])

    @pl.when(i >= nu_ref[0])
    def _():
        ys_ref[...] = jnp.zeros_like(ys_ref)


def _ffn(block_e, n_used, xs, w_gate, w_up, w_down):
    p_len = xs.shape[0]
    nblk = p_len // _FFN_BLOCK
    grid_spec = pltpu.PrefetchScalarGridSpec(
        num_scalar_prefetch=2,
        grid=(nblk,),
        in_specs=[pl.BlockSpec((_FFN_BLOCK, D_MODEL), lambda i, be, nu: (i, 0)),
                  pl.BlockSpec((None, D_MODEL, D_EXPERT), lambda i, be, nu: (be[i], 0, 0)),
                  pl.BlockSpec((None, D_MODEL, D_EXPERT), lambda i, be, nu: (be[i], 0, 0)),
                  pl.BlockSpec((None, D_EXPERT, D_MODEL), lambda i, be, nu: (be[i], 0, 0))],
        out_specs=pl.BlockSpec((_FFN_BLOCK, D_MODEL), lambda i, be, nu: (i, 0)),
        scratch_shapes=[pltpu.VMEM((D_MODEL, D_EXPERT), BF16), pltpu.VMEM((D_MODEL, D_EXPERT), BF16),
                        pltpu.VMEM((D_EXPERT, D_MODEL), BF16)])
    return pl.pallas_call(
        _ffn_body,
        grid_spec=grid_spec,
        out_shape=jax.ShapeDtypeStruct((p_len, D_MODEL), F32),
        compiler_params=_cparams(("arbitrary",)),
        name="ffn",
    )(block_e, n_used, xs, w_gate, w_up, w_down)


def _combine_body(dest_ref, h2_ref, rt_ref, ys_ref, g_ref, out_ref, buf, sem):
    ts = h2_ref.shape[0]

    def row_copy(k, t, d):
        return pltpu.make_async_copy(ys_ref.at[pl.ds(d, 1)], buf.at[k, pl.ds(t, 1)], sem)

    def issue(t, carry):
        row_copy(0, t, dest_ref[0, t]).start()
        row_copy(1, t, dest_ref[1, t]).start()
        return carry

    lax.fori_loop(0, ts, issue, 0, unroll=_ROW_UNROLL)

    def drain(t, carry):
        row_copy(0, 0, 0).wait()
        row_copy(1, 0, 0).wait()
        return carry

    lax.fori_loop(0, ts, drain, 0, unroll=_ROW_UNROLL)
    rt = rt_ref[...]
    w1 = rt[:, _R_W1:_R_W1 + 1]
    w2 = rt[:, _R_W2:_R_W2 + 1]
    h3 = h2_ref[...] + w1 * buf[0] + w2 * buf[1]
    out_ref[...] = _rmsnorm(h3, g_ref[...])


def _combine(dest, h2, route2d, ys, ln_final, ts):
    n = h2.shape[0]
    return pl.pallas_call(
        _combine_body,
        grid=(n // ts,),
        in_specs=[pl.BlockSpec((2, ts), lambda i: (0, i), memory_space=pltpu.SMEM),
                  pl.BlockSpec((ts, D_MODEL), lambda i: (i, 0)),
                  pl.BlockSpec((ts, LANES), lambda i: (i, 0)),
                  pl.BlockSpec(memory_space=pl.ANY),
                  pl.BlockSpec((1, D_MODEL), lambda i: (0, 0))],
        out_specs=pl.BlockSpec((ts, D_MODEL), lambda i: (i, 0)),
        out_shape=jax.ShapeDtypeStruct((n, D_MODEL), F32),
        scratch_shapes=[pltpu.VMEM((2, ts, D_MODEL), F32), pltpu.SemaphoreType.DMA(())],
        compiler_params=_cparams(("arbitrary",)),
        name="combine",
    )(dest, h2, route2d, ys, ln_final.reshape(1, D_MODEL))


def _moe(h2, hn2, route2d, p):
    n = h2.shape[0]
    rank, counts = _route(route2d, tr=min(512, n))
    counts = counts[0, :N_EXPERTS].astype(I32)
    blk = _FFN_BLOCK
    padded = (counts + blk - 1) // blk * blk
    pends = jnp.cumsum(padded)
    pstart = pends - padded
    nblk = 2 * n // blk + N_EXPERTS
    starts = jnp.arange(nblk, dtype=I32) * blk
    block_e = jnp.minimum(jnp.sum((pends[None, :] <= starts[:, None]).astype(I32), axis=1), N_EXPERTS - 1)
    n_used = (pends[-1:] // blk).astype(I32)
    eid = route2d[:, _R_E1:_R_E2 + 1].astype(I32)
    dest = (pstart[eid] + rank[:, 0:2].astype(I32)).T
    xs = _scatter(dest, hn2, jnp.zeros((nblk * blk, D_MODEL), F32), ts=min(512, n))
    ys = _ffn(block_e, n_used, xs, p["moe_w_gate"], p["moe_w_up"], p["moe_w_down"])
    return _combine(dest, h2, route2d, ys, p["ln_final"], ts=min(512, n))


def _encode_group(x, mem, p):
    bsz, l, _ = x.shape
    n = bsz * l
    r3 = lambda t: t.reshape(bsz, l, t.shape[-1])
    aq, ak, av, ar, bqkv, bg, ma, mb, lr, gates = _in_proj(
        x.reshape(n, D_MODEL), p["ln_mix"], p["w_main"], p["w_lr"], p["w_g"], tm=min(512, n))
    cq, ck, cv = _gdn_prep(r3(bqkv), p["gdn_conv"], tb=min(512, l))
    oaf, oab = _gla(r3(aq), r3(ak), r3(av), r3(lr), p["gla_w_up"], p["gla_b_up"], tb=min(512, l))
    obf, obb = _gdn(cq, ck, cv, r3(gates), p["gdn_params"], tb=min(512, l))
    kmem, vmem = _mem_kv(mem, p["ln_mem"], p["xa_wk"], p["xa_wv"])
    h2, hn2, route = _post(x, oaf, oab, obf, obb, r3(ar), r3(bg), r3(ma), r3(mb), kmem, vmem, p, tm=min(512, l))
    out = _moe(h2.reshape(n, D_MODEL), hn2.reshape(n, D_MODEL), route.reshape(n, LANES), p)
    return out.reshape(bsz, l, D_MODEL)


def kernel(x_prompt, x_sample, mem_prompt, mem_sample, ln_mix, w_in, gla_w_up_f, gla_b_up_f, gla_w_up_b, gla_b_up_b, gla_onorm, gdn_conv, gdn_a_log_f, gdn_dt_bias_f, gdn_a_log_b, gdn_dt_bias_b, gdn_onorm, w_out, ln_xa, ln_mem, xa_wq, xa_wk, xa_wv, xa_wo, ln_moe, moe_w_group, moe_b_group, moe_w_expert, moe_b_expert, moe_w_gate, moe_w_up, moe_w_down, ln_final):
    w_main, w_lr, w_g = _pack_in_weights(w_in[0])
    gla_w_up, gla_b_up = _pack_gla_up(gla_w_up_f[0], gla_b_up_f[0], gla_w_up_b[0], gla_b_up_b[0])
    vec = lambda t: t.reshape(1, D_MODEL).astype(F32)
    w_r = jnp.concatenate([moe_w_group[0], moe_w_expert[0],
                           jnp.zeros((D_MODEL, LANES - N_GROUPS - N_EXPERTS), F32)], axis=1)
    w_r_hi = w_r.astype(BF16)
    b_r = jnp.concatenate([moe_b_group[0], moe_b_expert[0], jnp.zeros((LANES - N_GROUPS - N_EXPERTS,), F32)])
    p = dict(ln_mix=ln_mix[0], w_main=w_main, w_lr=w_lr, w_g=w_g, gla_w_up=gla_w_up, gla_b_up=gla_b_up,
             gdn_conv=gdn_conv[0],
             gdn_params=_pack_gdn_params(gdn_a_log_f[0], gdn_dt_bias_f[0], gdn_a_log_b[0], gdn_dt_bias_b[0]),
             gla_onorm=vec(jnp.tile(gla_onorm[0], N_HEADS)), gdn_onorm=vec(jnp.tile(gdn_onorm[0], N_HEADS)),
             w_out=w_out[0].astype(BF16), ln_xa=vec(ln_xa[0]), ln_mem=ln_mem[0],
             xa_wq=xa_wq[0].astype(BF16), xa_wk=xa_wk[0].astype(BF16), xa_wv=xa_wv[0].astype(BF16),
             xa_wo=xa_wo[0].astype(BF16), ln_moe=vec(ln_moe[0]),
             w_r_hi=w_r_hi, w_r_lo=(w_r - w_r_hi.astype(F32)).astype(BF16), b_r=b_r.reshape(1, LANES),
             moe_w_gate=moe_w_gate[0], moe_w_up=moe_w_up[0], moe_w_down=moe_w_down[0], ln_final=ln_final)
    return (_encode_group(x_prompt, mem_prompt, p), _encode_group(x_sample, mem_sample, p))
```
